```python
import math
import jax, jax.numpy as jnp
from jax import lax
import numpy as np

D_MODEL = 2048
BATCH = 4
SEQ = 2048
DEPTH = 1
DEC_BATCH = 128
DEC_SEQ = 1
PAST_LEN = 16384
PAGE_SIZE = 128

SSD_EXPAND = 2
D_INNER = SSD_EXPAND * D_MODEL
SSD_HEADDIM = 64
SSD_HEADS = D_INNER // SSD_HEADDIM
SSD_GROUPS = 8
SSD_STATE = 128
SSD_CONV = 4
SSD_CHUNK = 128
CONV_DIM = D_INNER + 2 * SSD_GROUPS * SSD_STATE
MLP_CHUNK = 128
D_GATE = D_MODEL
MLP_GROUPS = 16
MLP_GROUP_DIM = D_GATE // MLP_GROUPS
PEER_HEADS = 8
PEER_KEYS = 128
PEER_EXPERTS = PEER_KEYS * PEER_KEYS
PEER_DKEY = 256
PEER_TOPK = 16
PEER_BLOCK = 128
DN_ALPHA = (2.0 * DEPTH) ** 0.25
DN_BETA = (8.0 * DEPTH) ** -0.25
LN_EPS = 1e-5
RMS_EPS = 1e-5
D_IN_PROJ = D_INNER + CONV_DIM + SSD_HEADS + 2 * D_GATE + 2 * D_MODEL
IN_SPLITS = (D_INNER, D_INNER + CONV_DIM, D_INNER + CONV_DIM + SSD_HEADS,
             D_INNER + CONV_DIM + SSD_HEADS + 2 * D_GATE)

kernel_name = "hybrid_ssd_chunkmlp_peer_step"


def layer_norm(x, g, b):
    xf = x.astype(jnp.float32)
    mu = jnp.mean(xf, axis=-1, keepdims=True)
    xc = xf - mu
    var = jnp.mean(xc * xc, axis=-1, keepdims=True)
    return (xc * lax.rsqrt(var + LN_EPS) * g + b).astype(x.dtype)


def causal_conv(xbc, conv_state, conv_w, conv_b):
    L = xbc.shape[1]
    xp = jnp.concatenate([conv_state.astype(xbc.dtype), xbc], axis=1)
    out = conv_b + sum(xp[:, k:k + L] * conv_w[k] for k in range(SSD_CONV))
    return out, xp[:, L:]


def ssd_scan(xh, dt, A, Bm, Cm, init_state):
    b, L = xh.shape[:2]
    G, R, P, N = SSD_GROUPS, SSD_HEADS // SSD_GROUPS, SSD_HEADDIM, SSD_STATE
    Q = min(SSD_CHUNK, L)
    Lp = -(-L // Q) * Q
    nc = Lp // Q
    pad = Lp - L
    f32 = jnp.float32
    xh = jnp.pad(xh.astype(f32), ((0, 0), (0, pad), (0, 0), (0, 0)))
    dt = jnp.pad(dt.astype(f32), ((0, 0), (0, pad), (0, 0)))
    Bm = jnp.pad(Bm.astype(f32), ((0, 0), (0, pad), (0, 0), (0, 0)))
    Cm = jnp.pad(Cm.astype(f32), ((0, 0), (0, pad), (0, 0), (0, 0)))
    x_c = xh.reshape(b, nc, Q, G, R, P)
    dt_c = dt.reshape(b, nc, Q, G, R)
    B_c = Bm.reshape(b, nc, Q, G, N)
    C_c = Cm.reshape(b, nc, Q, G, N)
    acum = jnp.cumsum(dt_c * A.reshape(G, R), axis=2)
    xdt = x_c * dt_c[..., None]
    seg = acum[:, :, :, None] - acum[:, :, None, :]
    causal = jnp.tril(jnp.ones((Q, Q), dtype=bool))[None, None, :, :, None, None]
    decay = jnp.exp(jnp.where(causal, seg, -jnp.inf))
    cb = jnp.einsum('bcign,bcjgn->bcijg', C_c, B_c)
    y_diag = jnp.einsum('bcijg,bcijgr,bcjgrp->bcigrp', cb, decay, xdt)
    decay_to_end = jnp.exp(acum[:, :, -1:] - acum)
    chunk_states = jnp.einsum('bcjgn,bcjgr,bcjgrp->bcgrpn', B_c, decay_to_end, xdt)
    chunk_decay = jnp.exp(acum[:, :, -1])

    def step(carry, inp):
        st, dcy = inp
        return carry * dcy[..., None, None] + st, carry

    init = init_state.astype(f32).reshape(b, G, R, P, N)
    final, states_in = lax.scan(step, init, (jnp.moveaxis(chunk_states, 1, 0),
                                             jnp.moveaxis(chunk_decay, 1, 0)))
    states_in = jnp.moveaxis(states_in, 0, 1)
    y_off = jnp.einsum('bcign,bcgrpn,bcigr->bcigrp', C_c, states_in, jnp.exp(acum))
    y = (y_diag + y_off).reshape(b, Lp, SSD_HEADS, P)[:, :L]
    return y, final.reshape(b, SSD_HEADS, P, N).astype(init_state.dtype)


def ssd_branch(z, xbc, dt_raw, conv_state, ssm_state, conv_w, conv_b, dt_bias, a_log, d_skip, norm_w):
    b, L = z.shape[:2]
    xbc_c, new_conv = causal_conv(xbc, conv_state, conv_w, conv_b)
    xbc_c = jax.nn.silu(xbc_c)
    xs, Bm, Cm = jnp.split(xbc_c, (D_INNER, D_INNER + SSD_GROUPS * SSD_STATE), axis=-1)
    dt = jax.nn.softplus(dt_raw.astype(jnp.float32) + dt_bias.astype(jnp.float32))
    A = -jnp.exp(a_log.astype(jnp.float32))
    xh = xs.reshape(b, L, SSD_HEADS, SSD_HEADDIM)
    y, new_ssm = ssd_scan(xh, dt, A, Bm.reshape(b, L, SSD_GROUPS, SSD_STATE),
                          Cm.reshape(b, L, SSD_GROUPS, SSD_STATE), ssm_state)
    y = y + d_skip.astype(jnp.float32)[:, None] * xh.astype(jnp.float32)
    yg = (y.reshape(b, L, D_INNER) * jax.nn.silu(z.astype(jnp.float32)))
    yg = yg.reshape(b, L, SSD_GROUPS, D_INNER // SSD_GROUPS)
    yg = yg * lax.rsqrt(jnp.mean(yg * yg, axis=-1, keepdims=True) + RMS_EPS)
    y_out = (yg.reshape(b, L, D_INNER) * norm_w).astype(z.dtype)
    return y_out, new_conv, new_ssm


def chunk_sgu_branch(uv, ln_g, ln_b, w_s, b_s):
    b, L = uv.shape[:2]
    uv = jax.nn.gelu(uv, approximate=False)
    u, v = jnp.split(uv, 2, axis=-1)
    v = layer_norm(v, ln_g, ln_b)
    Lp = -(-L // MLP_CHUNK) * MLP_CHUNK
    nc = Lp // MLP_CHUNK
    vc = jnp.pad(v, ((0, 0), (0, Lp - L), (0, 0))).reshape(b, nc, MLP_CHUNK, MLP_GROUPS, MLP_GROUP_DIM)
    w = w_s * jnp.tril(jnp.ones((MLP_CHUNK, MLP_CHUNK), dtype=w_s.dtype))
    s = jnp.einsum('gij,bcjgd->bcigd', w, vc) + b_s.T[None, None, :, :, None]
    s = s.reshape(b, Lp, D_GATE)[:, :L]
    n_last = L - MLP_CHUNK * ((L - 1) // MLP_CHUNK)
    return u * s, v[:, L - n_last:]


def peer(h, w_q, sub_keys, u_tab, v_tab):
    b, L, D = h.shape
    T = b * L
    t = h.reshape(T, D)
    K = PEER_TOPK
    q = (t @ w_q).astype(jnp.float32).reshape(T, PEER_HEADS, 2, PEER_DKEY // 2)
    s = jnp.einsum('thsd,hskd->thsk', q, sub_keys.astype(jnp.float32))
    top_s, top_i = lax.top_k(s, K)
    cand = (top_s[:, :, 0, :, None] + top_s[:, :, 1, None, :]).reshape(T, PEER_HEADS, K * K)
    best_s, best_c = lax.top_k(cand, K)
    i1 = jnp.take_along_axis(top_i[:, :, 0], best_c // K, axis=-1)
    i2 = jnp.take_along_axis(top_i[:, :, 1], best_c % K, axis=-1)
    expert = i1 * PEER_KEYS + i2
    gate = jax.nn.softmax(best_s, axis=-1).astype(h.dtype)
    nb = -(-T // PEER_BLOCK)
    Tp = nb * PEER_BLOCK
    pad = Tp - T
    t_b = jnp.pad(t, ((0, pad), (0, 0))).reshape(nb, PEER_BLOCK, D)
    e_b = jnp.pad(expert, ((0, pad), (0, 0), (0, 0))).reshape(nb, PEER_BLOCK, PEER_HEADS, K)
    g_b = jnp.pad(gate, ((0, pad), (0, 0), (0, 0))).reshape(nb, PEER_BLOCK, PEER_HEADS, K)

    def block(args):
        tb, eb, gb = args
        ue = u_tab[eb]
        ve = v_tab[eb]
        act = jax.nn.gelu(jnp.einsum('td,thkd->thk', tb, ue), approximate=False)
        return jnp.einsum('thk,thkd->td', gb * act, ve)

    out = lax.map(block, (t_b, e_b, g_b))
    return out.reshape(Tp, D)[:T].reshape(b, L, D)


def trunk(x, conv_states, ssm_states, w_in, conv_w, conv_b, dt_bias, a_log, d_skip, ssd_norm_w,
          sgu_ln_g, sgu_ln_b, sgu_w, sgu_b, w_branch_a, w_branch_b, w_out, ln1_g, ln1_b,
          peer_wq, peer_keys, peer_u, peer_v, ln2_g, ln2_b):
    new_conv, new_ssm, new_v = [], [], []
    for i in range(DEPTH):
        proj = x @ w_in[i]
        z, xbc, dt_raw, uv, gates = jnp.split(proj, IN_SPLITS, axis=-1)
        y_a, c_i, s_i = ssd_branch(z, xbc, dt_raw, conv_states[i], ssm_states[i], conv_w[i], conv_b[i],
                                   dt_bias[i], a_log[i], d_skip[i], ssd_norm_w[i])
        y_b, v_i = chunk_sgu_branch(uv, sgu_ln_g[i], sgu_ln_b[i], sgu_w[i], sgu_b[i])
        g_a, g_b = jnp.split(jax.nn.sigmoid(gates), 2, axis=-1)
        merged = g_a * (y_a @ w_branch_a[i]) + g_b * (y_b @ w_branch_b[i])
        mix = merged @ w_out[i]
        h = layer_norm(DN_ALPHA * x + mix, ln1_g[i], ln1_b[i])
        x = layer_norm(DN_ALPHA * h + peer(h, peer_wq[i], peer_keys[i], peer_u[i], peer_v[i]),
                       ln2_g[i], ln2_b[i])
        new_conv.append(c_i)
        new_ssm.append(s_i)
        new_v.append(v_i)
    return x, jnp.stack(new_conv), jnp.stack(new_ssm), jnp.stack(new_v)


def setup_inputs(seed: int = 0) -> dict:
    key = jax.random.key(seed)
    ks = jax.random.split(key, 32)
    nrm = lambda k, shape, s: jax.random.normal(k, shape, jnp.float32) * s
    dt0 = jnp.exp(jax.random.uniform(ks[5], (DEPTH, SSD_HEADS), jnp.float32,
                                     minval=math.log(1e-3), maxval=math.log(1e-1)))
    return {
        "x_prompt": nrm(ks[0], (BATCH, SEQ, D_MODEL), 1.0),
        "x_sample": nrm(ks[1], (DEC_BATCH, DEC_SEQ, D_MODEL), 1.0),
        "state_conv": nrm(ks[2], (DEPTH, DEC_BATCH, SSD_CONV - 1, CONV_DIM), 1.0),
        "state_ssm": nrm(ks[3], (DEPTH, DEC_BATCH, SSD_HEADS, SSD_HEADDIM, SSD_STATE), 0.5),
        "w_in": nrm(ks[4], (DEPTH, D_MODEL, D_IN_PROJ), D_MODEL ** -0.5),
        "conv_w": nrm(ks[6], (DEPTH, SSD_CONV, CONV_DIM), SSD_CONV ** -0.5),
        "conv_b": nrm(ks[7], (DEPTH, CONV_DIM), 0.01),
        "dt_bias": dt0 + jnp.log(-jnp.expm1(-dt0)),
        "a_log": jnp.log(jax.random.uniform(ks[8], (DEPTH, SSD_HEADS), jnp.float32, minval=1.0, maxval=16.0)),
        "d_skip": 1.0 + nrm(ks[9], (DEPTH, SSD_HEADS), 0.1),
        "ssd_norm_w": 1.0 + nrm(ks[10], (DEPTH, D_INNER), 0.01),
        "sgu_ln_g": 1.0 + nrm(ks[11], (DEPTH, D_GATE), 0.01),
        "sgu_ln_b": nrm(ks[12], (DEPTH, D_GATE), 0.01),
        "sgu_w": nrm(ks[13], (DEPTH, MLP_GROUPS, MLP_CHUNK, MLP_CHUNK), MLP_CHUNK ** -0.5),
        "sgu_b": 1.0 + nrm(ks[14], (DEPTH, MLP_GROUPS, MLP_CHUNK), 0.01),
        "w_branch_a": nrm(ks[15], (DEPTH, D_INNER, D_MODEL), D_INNER ** -0.5),
        "w_branch_b": nrm(ks[16], (DEPTH, D_GATE, D_MODEL), D_GATE ** -0.5),
        "w_out": nrm(ks[17], (DEPTH, D_MODEL, D_MODEL), DN_BETA * D_MODEL ** -0.5),
        "ln1_g": 1.0 + nrm(ks[18], (DEPTH, D_MODEL), 0.01),
        "ln1_b": nrm(ks[19], (DEPTH, D_MODEL), 0.01),
        "peer_wq": nrm(ks[20], (DEPTH, D_MODEL, PEER_HEADS * PEER_DKEY), D_MODEL ** -0.5),
        "peer_keys": nrm(ks[21], (DEPTH, PEER_HEADS, 2, PEER_KEYS, PEER_DKEY // 2), (PEER_DKEY // 2) ** -0.5),
        "peer_u": nrm(ks[22], (DEPTH, PEER_EXPERTS, D_MODEL), D_MODEL ** -0.5),
        "peer_v": nrm(ks[23], (DEPTH, PEER_EXPERTS, D_MODEL), DN_BETA),
        "ln2_g": 1.0 + nrm(ks[24], (DEPTH, D_MODEL), 0.01),
        "ln2_b": nrm(ks[25], (DEPTH, D_MODEL), 0.01),
    }


def reference(x_prompt, x_sample, state_conv, state_ssm, w_in, conv_w, conv_b, dt_bias, a_log, d_skip,
              ssd_norm_w, sgu_ln_g, sgu_ln_b, sgu_w, sgu_b, w_branch_a, w_branch_b, w_out, ln1_g, ln1_b,
              peer_wq, peer_keys, peer_u, peer_v, ln2_g, ln2_b):
    b_p = x_prompt.shape[0]
    conv0 = jnp.zeros((DEPTH, b_p, SSD_CONV - 1, CONV_DIM), x_prompt.dtype)
    ssm0 = jnp.zeros((DEPTH, b_p, SSD_HEADS, SSD_HEADDIM, SSD_STATE), state_ssm.dtype)
    y_prompt, conv_prompt, ssm_prompt, sgu_v_prompt = trunk(
        x_prompt, conv0, ssm0, w_in, conv_w, conv_b, dt_bias, a_log, d_skip, ssd_norm_w,
        sgu_ln_g, sgu_ln_b, sgu_w, sgu_b, w_branch_a, w_branch_b, w_out, ln1_g, ln1_b,
        peer_wq, peer_keys, peer_u, peer_v, ln2_g, ln2_b)
    y_sample, conv_sample, ssm_sample, sgu_v_sample = trunk(
        x_sample, state_conv, state_ssm, w_in, conv_w, conv_b, dt_bias, a_log, d_skip, ssd_norm_w,
        sgu_ln_g, sgu_ln_b, sgu_w, sgu_b, w_branch_a, w_branch_b, w_out, ln1_g, ln1_b,
        peer_wq, peer_keys, peer_u, peer_v, ln2_g, ln2_b)
    return (y_prompt, y_sample, conv_prompt, ssm_prompt, sgu_v_prompt, conv_sample, ssm_sample, sgu_v_sample)
```

```python
import functools
import math

import jax
import jax.numpy as jnp
from jax import lax
from jax.experimental import pallas as pl
from jax.experimental.pallas import tpu as pltpu

F32 = jnp.float32
BF16 = jnp.bfloat16

D_MODEL = 2048
SEQ = 2048
BATCH = 4
DEC_BATCH = 128
T_PROMPT = BATCH * SEQ
T_ALL = T_PROMPT + DEC_BATCH
CHUNK = 128
N_CHUNKS = SEQ // CHUNK
SAMPLE_BLOCK = T_PROMPT // CHUNK

D_INNER = 4096
HEADDIM = 64
HEADS = 64
GROUPS = 8
HEADS_PER_GROUP = HEADS // GROUPS
STATE = 128
CONV_K = 4
CONV_DIM = D_INNER + 2 * GROUPS * STATE
B_OFF = D_INNER
C_OFF = D_INNER + GROUPS * STATE
HEAD_PAIRS = HEADS // 2
DT_PAD = 128
D_GATE = 2048
MLP_GROUPS = 16
PEER_HEADS = 8
PEER_KEYS = 128
PEER_EXPERTS = PEER_KEYS * PEER_KEYS
PEER_TOPK = 16
DN_ALPHA = 2.0 ** 0.25
LN_EPS = 1e-5
RMS_EPS = 1e-5
NEG_INF = float("-inf")

TOKEN_TILE = 640
MM_TN = 512
PEER_EB = 512
VMEM_LIMIT = 56 * 1024 * 1024


def _params(*sem):
    return pltpu.CompilerParams(dimension_semantics=sem, vmem_limit_bytes=VMEM_LIMIT)


def _gelu(x):
    return 0.5 * x * (1.0 + lax.erf(x * (1.0 / math.sqrt(2.0))))


def _silu(x):
    return x * jax.nn.sigmoid(x)


def _identity(x):
    return x


def _softplus(x):
    return jnp.maximum(x, 0.0) + jnp.log1p(jnp.exp(-jnp.abs(x)))


def _layer_norm(x, g, b):
    mu = jnp.mean(x, axis=-1, keepdims=True)
    xc = x - mu
    var = jnp.mean(xc * xc, axis=-1, keepdims=True)
    return xc * lax.rsqrt(var + LN_EPS) * g + b


def _dot(a, b):
    return jnp.dot(a, b, preferred_element_type=F32)


def _dot_f32(a, b):
    return jnp.dot(a, b, preferred_element_type=F32, precision=lax.Precision.HIGHEST)


def _mm_kernel(a_ref, b_ref, o_ref, *, act):
    o_ref[...] = act(_dot(a_ref[...], b_ref[...])).astype(o_ref.dtype)


def _matmul(a, b, *, tm, tn, act=_identity, out_dtype=F32):
    m, k = a.shape
    n = b.shape[1]
    return pl.pallas_call(
        functools.partial(_mm_kernel, act=act),
        grid=(m // tm, n // tn),
        in_specs=[pl.BlockSpec((tm, k), lambda i, j: (i, 0)),
                  pl.BlockSpec((k, tn), lambda i, j: (0, j))],
        out_specs=pl.BlockSpec((tm, tn), lambda i, j: (i, j)),
        out_shape=jax.ShapeDtypeStruct((m, n), out_dtype),
        compiler_params=_params("parallel", "parallel"),
    )(a, b)


def _ssd_gate_norm(y, xs, zs, dexp, normw):
    yg = (y + dexp * xs) * zs
    gw = D_INNER // GROUPS
    outs = []
    for g in range(GROUPS):
        blk = yg[:, g * gw:(g + 1) * gw]
        ms = jnp.mean(blk * blk, axis=-1, keepdims=True)
        outs.append(blk * lax.rsqrt(ms + RMS_EPS) * normw[:, g * gw:(g + 1) * gw])
    return outs


def _ssd_prompt_kernel(xbc_ref, dtr_ref, zs_ref, convw_ref, convb_ref, dtb_ref, alog_ref,
                       dexp_ref, normw_ref,
                       y_ref, convst_ref, ssm_ref,
                       xext, xc_s, rexp_s, rcol_s, colb_s, acumt_s, xdtb_s, xdte_s, stt_s, ysc_s):
    c = pl.program_id(1)

    @pl.when(c == 0)
    def _():
        xext[0:8, :] = jnp.zeros((8, CONV_DIM), F32)
        stt_s[...] = jnp.zeros(stt_s.shape, F32)
        hrow = lax.broadcasted_iota(jnp.int32, (DT_PAD, D_INNER), 0)
        col = lax.broadcasted_iota(jnp.int32, (DT_PAD, D_INNER), 1)
        rexp_s[...] = jnp.where(lax.shift_right_logical(col, 6) == hrow, 1.0, 0.0).astype(F32)
        hrow = lax.broadcasted_iota(jnp.int32, (DT_PAD, HEADS * CHUNK), 0)
        col = lax.broadcasted_iota(jnp.int32, (DT_PAD, HEADS * CHUNK), 1)
        rcol_s[...] = jnp.where(lax.shift_right_logical(col, 7) == hrow, 1.0, 0.0).astype(F32)

    xext[8:8 + CHUNK, :] = xbc_ref[...]
    slab = 512
    for s0 in range(0, CONV_DIM, slab):
        sl = slice(s0, s0 + slab)
        acc = convb_ref[:, sl] + convw_ref[3:4, sl] * xext[8:8 + CHUNK, sl]
        acc = acc + convw_ref[2:3, sl] * xext[7:7 + CHUNK, sl]
        acc = acc + convw_ref[1:2, sl] * xext[6:6 + CHUNK, sl]
        acc = acc + convw_ref[0:1, sl] * xext[5:5 + CHUNK, sl]
        xc_s[:, sl] = _silu(acc)
    convst_ref[0] = xbc_ref[CHUNK - 3:CHUNK, :]
    xext[0:8, :] = xbc_ref[CHUNK - 8:CHUNK, :]

    dtv = _softplus(dtr_ref[...] + dtb_ref[...])
    a = dtv * (-jnp.exp(alog_ref[...]))
    row = lax.broadcasted_iota(jnp.int32, (CHUNK, CHUNK), 0)
    colq = lax.broadcasted_iota(jnp.int32, (CHUNK, CHUNK), 1)
    causal = row >= colq
    acum = _dot_f32(jnp.where(causal, 1.0, 0.0).astype(F32), a)
    acumt_s[...] = acum.T
    colb_s[...] = _dot_f32(acum, rcol_s[...])
    dt_exp = _dot_f32(dtv, rexp_s[...])
    acum_exp = _dot_f32(acum, rexp_s[...])
    last_exp = acum_exp[CHUNK - 1:CHUNK, :]
    xdt = xc_s[:, 0:D_INNER] * dt_exp
    xdtb_s[...] = xdt.astype(BF16)
    xdte_s[...] = (xdt * jnp.exp(last_exp - acum_exp)).astype(BF16)
    chunk_decay = jnp.exp(last_exp)

    lane = lax.broadcasted_iota(jnp.int32, (CHUNK, 128), 1)
    lo = lane < HEADDIM
    zero_b = jnp.zeros((CHUNK, 128), BF16)
    for g in range(GROUPS):
        bg = xc_s[:, B_OFF + g * STATE:B_OFF + (g + 1) * STATE]
        cg = xc_s[:, C_OFF + g * STATE:C_OFF + (g + 1) * STATE]
        bgb = bg.astype(BF16)
        cb = lax.dot_general(cg.astype(BF16), bgb, (((1,), (1,)), ((), ())),
                             preferred_element_type=F32)
        for pr in range(HEADS_PER_GROUP // 2):
            hp = g * (HEADS_PER_GROUP // 2) + pr
            psl = slice(hp * 128, (hp + 1) * 128)
            lhs = []
            for hh in (2 * hp, 2 * hp + 1):
                colb = colb_s[:, hh * CHUNK:(hh + 1) * CHUNK]
                seg = colb - acumt_s[hh:hh + 1, :]
                decay = jnp.exp(jnp.where(causal, seg, NEG_INF))
                lhs.append((cb * decay).astype(BF16))
            for hh in (2 * hp, 2 * hp + 1):
                colb = colb_s[:, hh * CHUNK:(hh + 1) * CHUNK]
                lhs.append((cg * jnp.exp(colb)).astype(BF16))
            xpair = xdtb_s[:, psl]
            spair = stt_s[hp].astype(BF16)
            rhs = jnp.concatenate([jnp.where(lo, xpair, zero_b), jnp.where(lo, zero_b, xpair),
                                   jnp.where(lo, spair, zero_b), jnp.where(lo, zero_b, spair)],
                                  axis=0)
            ysc_s[:, psl] = _dot(jnp.concatenate(lhs, axis=1), rhs)
            upd = lax.dot_general(bgb, xdte_s[:, psl], (((0,), (0,)), ((), ())),
                                  preferred_element_type=F32)
            stt_s[hp] = stt_s[hp] * chunk_decay[:, psl] + upd

    outs = _ssd_gate_norm(ysc_s[...], xc_s[:, 0:D_INNER], zs_ref[...], dexp_ref[...], normw_ref[...])
    gw = D_INNER // GROUPS
    for g in range(GROUPS):
        y_ref[:, g * gw:(g + 1) * gw] = outs[g].astype(y_ref.dtype)

    @pl.when(c == N_CHUNKS - 1)
    def _():
        for hp in range(HEAD_PAIRS):
            st = stt_s[hp].T
            ssm_ref[0, 2 * hp] = st[0:HEADDIM, :]
            ssm_ref[0, 2 * hp + 1] = st[HEADDIM:2 * HEADDIM, :]


def _ssd_prompt(xbc, dtr, zs, convw, convb, dtb, alog, dexp, normw):
    rowblk = lambda b, c: (b * N_CHUNKS + c, 0)
    const = lambda b, c: (0, 0)
    return pl.pallas_call(
        _ssd_prompt_kernel,
        grid=(BATCH, N_CHUNKS),
        in_specs=[pl.BlockSpec((CHUNK, CONV_DIM), rowblk),
                  pl.BlockSpec((CHUNK, DT_PAD), rowblk),
                  pl.BlockSpec((CHUNK, D_INNER), rowblk),
                  pl.BlockSpec((CONV_K, CONV_DIM), const),
                  pl.BlockSpec((1, CONV_DIM), const),
                  pl.BlockSpec((1, DT_PAD), const),
                  pl.BlockSpec((1, DT_PAD), const),
                  pl.BlockSpec((1, D_INNER), const),
                  pl.BlockSpec((1, D_INNER), const)],
        out_specs=[pl.BlockSpec((CHUNK, D_INNER), rowblk),
                   pl.BlockSpec((1, CONV_K - 1, CONV_DIM), lambda b, c: (b, 0, 0)),
                   pl.BlockSpec((1, HEADS, HEADDIM, STATE), lambda b, c: (b, 0, 0, 0))],
        out_shape=[jax.ShapeDtypeStruct((T_ALL, D_INNER), BF16),
                   jax.ShapeDtypeStruct((BATCH, CONV_K - 1, CONV_DIM), F32),
                   jax.ShapeDtypeStruct((BATCH, HEADS, HEADDIM, STATE), F32)],
        scratch_shapes=[pltpu.VMEM((8 + CHUNK, CONV_DIM), F32),
                        pltpu.VMEM((CHUNK, CONV_DIM), F32),
                        pltpu.VMEM((DT_PAD, D_INNER), F32),
                        pltpu.VMEM((DT_PAD, HEADS * CHUNK), F32),
                        pltpu.VMEM((CHUNK, HEADS * CHUNK), F32),
                        pltpu.VMEM((DT_PAD, CHUNK), F32),
                        pltpu.VMEM((CHUNK, D_INNER), BF16),
                        pltpu.VMEM((CHUNK, D_INNER), BF16),
                        pltpu.VMEM((HEAD_PAIRS, STATE, 128), F32),
                        pltpu.VMEM((CHUNK, D_INNER), F32)],
        compiler_params=_params("arbitrary", "arbitrary"),
    )(xbc, dtr, zs, convw, convb, dtb, alog, dexp, normw)


def _sgu_prompt_kernel(uvg_ref, lng_ref, lnb_ref, w_ref, bfull_ref, yb_ref, v_ref):
    vn = _layer_norm(uvg_ref[:, D_GATE:2 * D_GATE], lng_ref[...], lnb_ref[...])
    v_ref[0] = vn
    row = lax.broadcasted_iota(jnp.int32, (CHUNK, CHUNK), 0)
    col = lax.broadcasted_iota(jnp.int32, (CHUNK, CHUNK), 1)
    causal = row >= col
    for g in range(MLP_GROUPS):
        sl = slice(g * 128, (g + 1) * 128)
        wg = jnp.where(causal, w_ref[g], 0.0).astype(BF16)
        s = _dot(wg, vn[:, sl].astype(BF16)) + bfull_ref[:, sl]
        yb_ref[:, sl] = (uvg_ref[:, sl] * s).astype(yb_ref.dtype)


def _sgu_prompt(uvg, lng, lnb, w, bfull):
    const2 = lambda i: (0, 0)
    return pl.pallas_call(
        _sgu_prompt_kernel,
        grid=(BATCH * N_CHUNKS,),
        in_specs=[pl.BlockSpec((CHUNK, 2 * D_GATE), lambda i: (i, 0)),
                  pl.BlockSpec((1, D_GATE), const2),
                  pl.BlockSpec((1, D_GATE), const2),
                  pl.BlockSpec((MLP_GROUPS, CHUNK, CHUNK), lambda i: (0, 0, 0)),
                  pl.BlockSpec((CHUNK, D_GATE), const2)],
        out_specs=[pl.BlockSpec((CHUNK, D_GATE), lambda i: (i, 0)),
                   pl.BlockSpec((1, CHUNK, D_GATE), lambda i: (i // N_CHUNKS, 0, 0))],
        out_shape=[jax.ShapeDtypeStruct((T_ALL, D_GATE), BF16),
                   jax.ShapeDtypeStruct((BATCH, CHUNK, D_GATE), F32)],
        compiler_params=_params("arbitrary"),
    )(uvg, lng, lnb, w, bfull)


def _sample_pre_kernel(xbc_ref, sc_ref, dtr_ref, convw_ref, convb_ref, dtb_ref, alog_ref,
                       convnew_ref, xs_ref, xdt_ref, bm_ref, cm_ref, dec_ref):
    x = xbc_ref[...]
    acc = convb_ref[...] + convw_ref[3:4, :] * x
    acc = acc + convw_ref[2:3, :] * sc_ref[2]
    acc = acc + convw_ref[1:2, :] * sc_ref[1]
    acc = acc + convw_ref[0:1, :] * sc_ref[0]
    xc = _silu(acc)
    convnew_ref[0] = sc_ref[1]
    convnew_ref[1] = sc_ref[2]
    convnew_ref[2] = x
    dtv = _softplus(dtr_ref[...] + dtb_ref[...])
    dec_ref[...] = jnp.exp(dtv * (-jnp.exp(alog_ref[...])))
    hrow = lax.broadcasted_iota(jnp.int32, (DT_PAD, D_INNER), 0)
    col = lax.broadcasted_iota(jnp.int32, (DT_PAD, D_INNER), 1)
    rexp = jnp.where(lax.shift_right_logical(col, 6) == hrow, 1.0, 0.0).astype(F32)
    xs = xc[:, 0:D_INNER]
    xs_ref[...] = xs
    xdt_ref[...] = xs * _dot_f32(dtv, rexp)
    bm_ref[...] = xc[:, B_OFF:C_OFF]
    cm_ref[...] = xc[:, C_OFF:CONV_DIM]


def _sample_pre(xbc, sc_t, dtr, convw, convb, dtb, alog):
    blk = lambda i: (SAMPLE_BLOCK, 0)
    const = lambda i: (0, 0)
    return pl.pallas_call(
        _sample_pre_kernel,
        grid=(1,),
        in_specs=[pl.BlockSpec((DEC_BATCH, CONV_DIM), blk),
                  pl.BlockSpec((CONV_K - 1, DEC_BATCH, CONV_DIM), lambda i: (0, 0, 0)),
                  pl.BlockSpec((DEC_BATCH, DT_PAD), blk),
                  pl.BlockSpec((CONV_K, CONV_DIM), const),
                  pl.BlockSpec((1, CONV_DIM), const),
                  pl.BlockSpec((1, DT_PAD), const),
                  pl.BlockSpec((1, DT_PAD), const)],
        out_specs=[pl.BlockSpec((CONV_K - 1, DEC_BATCH, CONV_DIM), lambda i: (0, 0, 0)),
                   pl.BlockSpec((DEC_BATCH, D_INNER), const),
                   pl.BlockSpec((DEC_BATCH, D_INNER), const),
                   pl.BlockSpec((DEC_BATCH, GROUPS * STATE), const),
                   pl.BlockSpec((DEC_BATCH, GROUPS * STATE), const),
                   pl.BlockSpec((DEC_BATCH, DT_PAD), const)],
        out_shape=[jax.ShapeDtypeStruct((CONV_K - 1, DEC_BATCH, CONV_DIM), F32),
                   jax.ShapeDtypeStruct((DEC_BATCH, D_INNER), F32),
                   jax.ShapeDtypeStruct((DEC_BATCH, D_INNER), F32),
                   jax.ShapeDtypeStruct((DEC_BATCH, GROUPS * STATE), F32),
                   jax.ShapeDtypeStruct((DEC_BATCH, GROUPS * STATE), F32),
                   jax.ShapeDtypeStruct((DEC_BATCH, DT_PAD), F32)],
        compiler_params=_params("arbitrary"),
    )(xbc, sc_t, dtr, convw, convb, dtb, alog)


def _sample_state_kernel(dec_ref, xdtt_ref, bm_ref, cm_ref, st_ref, new_ref, yt_ref):
    s = pl.program_id(0)

    @pl.when(s == 0)
    def _():
        yt_ref[...] = jnp.zeros(yt_ref.shape, F32)

    sel = lax.broadcasted_iota(jnp.int32, (1, DEC_BATCH), 1) == s
    brow = bm_ref[pl.ds(s, 1), :]
    crow = cm_ref[pl.ds(s, 1), :]
    for h in range(HEADS):
        g = h // HEADS_PER_GROUP
        rows = slice(h * HEADDIM, (h + 1) * HEADDIM)
        xcol = jnp.sum(jnp.where(sel, xdtt_ref[rows, :], 0.0), axis=-1, keepdims=True)
        new = st_ref[0, rows, :] * dec_ref[s, h] + xcol * brow[:, g * STATE:(g + 1) * STATE]
        new_ref[0, rows, :] = new
        ycol = jnp.sum(new * crow[:, g * STATE:(g + 1) * STATE], axis=-1, keepdims=True)
        yt_ref[rows, :] = jnp.where(sel, ycol, yt_ref[rows, :])


def _sample_state(dec, xdtt, bm, cm, state):
    const = lambda s: (0, 0)
    return pl.pallas_call(
        _sample_state_kernel,
        grid=(DEC_BATCH,),
        in_specs=[pl.BlockSpec(memory_space=pltpu.SMEM),
                  pl.BlockSpec((D_INNER, DEC_BATCH), const),
                  pl.BlockSpec((DEC_BATCH, GROUPS * STATE), const),
                  pl.BlockSpec((DEC_BATCH, GROUPS * STATE), const),
                  pl.BlockSpec((1, D_INNER, STATE), lambda s: (s, 0, 0))],
        out_specs=[pl.BlockSpec((1, D_INNER, STATE), lambda s: (s, 0, 0)),
                   pl.BlockSpec((D_INNER, DEC_BATCH), const)],
        out_shape=[jax.ShapeDtypeStruct((DEC_BATCH, D_INNER, STATE), F32),
                   jax.ShapeDtypeStruct((D_INNER, DEC_BATCH), F32)],
        compiler_params=_params("arbitrary"),
    )(dec, xdtt, bm, cm, state)


def _sample_post_kernel(y_ref, xs_ref, zs_ref, dexp_ref, normw_ref, uvg_ref, lng_ref, lnb_ref,
                        w0_ref, b0_ref, ya_in, yb_in, ya_ref, yb_ref, v_ref):
    del ya_in, yb_in
    outs = _ssd_gate_norm(y_ref[...], xs_ref[...], zs_ref[...], dexp_ref[...], normw_ref[...])
    gw = D_INNER // GROUPS
    for g in range(GROUPS):
        ya_ref[:, g * gw:(g + 1) * gw] = outs[g].astype(ya_ref.dtype)
    vn = _layer_norm(uvg_ref[:, D_GATE:2 * D_GATE], lng_ref[...], lnb_ref[...])
    v_ref[...] = vn
    s = w0_ref[...] * vn + b0_ref[...]
    yb_ref[...] = (uvg_ref[:, 0:D_GATE] * s).astype(yb_ref.dtype)


def _sample_post(y, xs, zs, dexp, normw, uvg, lng, lnb, w0, b0, ya, yb):
    blk = lambda i: (SAMPLE_BLOCK, 0)
    const = lambda i: (0, 0)
    return pl.pallas_call(
        _sample_post_kernel,
        grid=(1,),
        in_specs=[pl.BlockSpec((DEC_BATCH, D_INNER), const),
                  pl.BlockSpec((DEC_BATCH, D_INNER), const),
                  pl.BlockSpec((DEC_BATCH, D_INNER), blk),
                  pl.BlockSpec((1, D_INNER), const),
                  pl.BlockSpec((1, D_INNER), const),
                  pl.BlockSpec((DEC_BATCH, 2 * D_GATE), blk),
                  pl.BlockSpec((1, D_GATE), const),
                  pl.BlockSpec((1, D_GATE), const),
                  pl.BlockSpec((1, D_GATE), const),
                  pl.BlockSpec((1, D_GATE), const),
                  pl.BlockSpec(memory_space=pl.ANY),
                  pl.BlockSpec(memory_space=pl.ANY)],
        out_specs=[pl.BlockSpec((DEC_BATCH, D_INNER), blk),
                   pl.BlockSpec((DEC_BATCH, D_GATE), blk),
                   pl.BlockSpec((DEC_BATCH, D_GATE), const)],
        out_shape=[jax.ShapeDtypeStruct((T_ALL, D_INNER), BF16),
                   jax.ShapeDtypeStruct((T_ALL, D_GATE), BF16),
                   jax.ShapeDtypeStruct((DEC_BATCH, D_GATE), F32)],
        input_output_aliases={10: 0, 11: 1},
        compiler_params=_params("arbitrary"),
    )(y, xs, zs, dexp, normw, uvg, lng, lnb, w0, b0, ya, yb)


def _merge_kernel(ya_ref, wa_ref, yb_ref, wb_ref, ga_ref, gb_ref, o_ref):
    o_ref[...] = (ga_ref[...] * _dot(ya_ref[...], wa_ref[...])
                  + gb_ref[...] * _dot(yb_ref[...], wb_ref[...])).astype(o_ref.dtype)


def _merge(ya, wa, yb, wb, gates):
    tm, tn = TOKEN_TILE, MM_TN
    nb = D_MODEL // tn
    return pl.pallas_call(
        _merge_kernel,
        grid=(T_ALL // tm, nb),
        in_specs=[pl.BlockSpec((tm, D_INNER), lambda i, j: (i, 0)),
                  pl.BlockSpec((D_INNER, tn), lambda i, j: (0, j)),
                  pl.BlockSpec((tm, D_GATE), lambda i, j: (i, 0)),
                  pl.BlockSpec((D_GATE, tn), lambda i, j: (0, j)),
                  pl.BlockSpec((tm, tn), lambda i, j: (i, j)),
                  pl.BlockSpec((tm, tn), lambda i, j: (i, j + nb))],
        out_specs=pl.BlockSpec((tm, tn), lambda i, j: (i, j)),
        out_shape=jax.ShapeDtypeStruct((T_ALL, D_MODEL), BF16),
        compiler_params=_params("parallel", "parallel"),
    )(ya, wa, yb, wb, gates, gates)


def _out_ln_kernel(m_ref, w_ref, x_ref, g_ref, b_ref, h_ref, hb_ref):
    h = _layer_norm(DN_ALPHA * x_ref[...] + _dot(m_ref[...], w_ref[...]), g_ref[...], b_ref[...])
    h_ref[...] = h
    hb_ref[...] = h.astype(BF16)


def _out_ln(merged, wout, x, g, b):
    tm = TOKEN_TILE
    const = lambda i: (0, 0)
    rows = lambda i: (i, 0)
    return pl.pallas_call(
        _out_ln_kernel,
        grid=(T_ALL // tm,),
        in_specs=[pl.BlockSpec((tm, D_MODEL), rows),
                  pl.BlockSpec((D_MODEL, D_MODEL), const),
                  pl.BlockSpec((tm, D_MODEL), rows),
                  pl.BlockSpec((1, D_MODEL), const),
                  pl.BlockSpec((1, D_MODEL), const)],
        out_specs=[pl.BlockSpec((tm, D_MODEL), rows), pl.BlockSpec((tm, D_MODEL), rows)],
        out_shape=[jax.ShapeDtypeStruct((T_ALL, D_MODEL), F32),
                   jax.ShapeDtypeStruct((T_ALL, D_MODEL), BF16)],
        compiler_params=_params("parallel"),
    )(merged, wout, x, g, b)


def _final_ln_kernel(h_ref, p_ref, g_ref, b_ref, o_ref):
    o_ref[...] = _layer_norm(DN_ALPHA * h_ref[...] + p_ref[...], g_ref[...], b_ref[...])


def _final_ln(h, p, g, b):
    tm = TOKEN_TILE
    const = lambda i: (0, 0)
    rows = lambda i: (i, 0)
    return pl.pallas_call(
        _final_ln_kernel,
        grid=(T_ALL // tm,),
        in_specs=[pl.BlockSpec((tm, D_MODEL), rows), pl.BlockSpec((tm, D_MODEL), rows),
                  pl.BlockSpec((1, D_MODEL), const), pl.BlockSpec((1, D_MODEL), const)],
        out_specs=pl.BlockSpec((tm, D_MODEL), rows),
        out_shape=jax.ShapeDtypeStruct((T_ALL, D_MODEL), F32),
        compiler_params=_params("parallel"),
    )(h, p, g, b)


def _peer_front_kernel(qt_ref, keys_ref, s1_ref, e1_ref, s2_ref, e2_ref, thr_ref):
    k = PEER_TOPK
    for h in range(PEER_HEADS):
        sc = []
        top = []
        for side in range(2):
            hs = 2 * h + side
            x = _dot(keys_ref[hs], qt_ref[hs * PEER_KEYS:(hs + 1) * PEER_KEYS, :])
            sc.append(x)
            vals = []
            for _ in range(k):
                m = jnp.max(x, axis=0, keepdims=True)
                vals.append(m)
                x = jnp.where(x == m, NEG_INF, x)
            top.append(vals)
        cands = [top[0][i] + top[1][j] for i in range(k) for j in range(k) if (i + 1) * (j + 1) <= k]
        pad = (-len(cands)) % 8
        tt = cands[0].shape[1]
        cand = jnp.concatenate(cands + [jnp.full((pad, tt), NEG_INF, F32)], axis=0)
        idx = lax.broadcasted_iota(jnp.int32, cand.shape, 0)
        best = []
        for _ in range(k):
            m = jnp.max(cand, axis=0, keepdims=True)
            best.append(m)
            first = jnp.min(jnp.where(cand == m, idx, cand.shape[0]), axis=0, keepdims=True)
            cand = jnp.where(idx == first, NEG_INF, cand)
        z = jnp.zeros_like(best[0])
        for bk in best:
            z = z + jnp.exp(bk - best[0])
        s1_ref[h] = sc[0]
        s2_ref[h] = sc[1]
        e1_ref[h] = jnp.exp(sc[0] - top[0][0]) * (1.0 / z)
        e2_ref[h] = jnp.exp(sc[1] - top[1][0])
        thr_ref[h:h + 1, :] = best[k - 1]


def _peer_front(qt, keys):
    tt = 128
    blk3 = pl.BlockSpec((PEER_HEADS, PEER_KEYS, tt), lambda i: (0, 0, i))
    shp3 = jax.ShapeDtypeStruct((PEER_HEADS, PEER_KEYS, T_ALL), F32)
    return pl.pallas_call(
        _peer_front_kernel,
        grid=(T_ALL // tt,),
        in_specs=[pl.BlockSpec((2 * PEER_HEADS * PEER_KEYS, tt), lambda i: (0, i)),
                  pl.BlockSpec((2 * PEER_HEADS, PEER_KEYS, PEER_KEYS), lambda i: (0, 0, 0))],
        out_specs=[blk3, blk3, blk3, blk3, pl.BlockSpec((PEER_HEADS, tt), lambda i: (0, i))],
        out_shape=[shp3, shp3, shp3, shp3, jax.ShapeDtypeStruct((PEER_HEADS, T_ALL), F32)],
        compiler_params=_params("parallel"),
    )(qt, keys)


def _peer_main_kernel(ht_ref, u_ref, v_ref, s1_ref, e1_ref, s2_ref, e2_ref, thr_ref, o_ref,
                      act_s, gt_s):
    j = pl.program_id(1)
    tt = ht_ref.shape[1]
    n_i1 = PEER_EB // PEER_KEYS
    act_s[...] = _dot(u_ref[...], ht_ref[...])

    def token_cols(tc, carry):
        cols = pl.ds(pl.multiple_of(tc * 128, 128), 128)
        for r in range(n_i1):
            i1 = j * n_i1 + r
            gate = jnp.zeros((PEER_KEYS, 128), F32)
            for h in range(PEER_HEADS):
                pair = s1_ref[i1, h:h + 1, cols] + s2_ref[h, :, cols]
                w = e1_ref[i1, h:h + 1, cols] * e2_ref[h, :, cols]
                gate = gate + jnp.where(pair >= thr_ref[h:h + 1, cols], w, 0.0)
            rows = slice(r * PEER_KEYS, (r + 1) * PEER_KEYS)
            gt_s[rows, cols] = (gate * _gelu(act_s[rows, cols])).astype(BF16)
        return carry

    lax.fori_loop(0, tt // 128, token_cols, 0)
    contrib = lax.dot_general(gt_s[...], v_ref[...], (((0,), (0,)), ((), ())),
                              preferred_element_type=F32)

    @pl.when(j == 0)
    def _():
        o_ref[...] = contrib

    @pl.when(j > 0)
    def _():
        o_ref[...] += contrib


def _peer_main(ht, u, v, s1, e1, s2, e2, thr):
    tt, eb = TOKEN_TILE, PEER_EB
    blk3 = pl.BlockSpec((PEER_HEADS, PEER_KEYS, tt), lambda i, j: (0, 0, i))
    blk1 = pl.BlockSpec((PEER_KEYS, PEER_HEADS, tt), lambda i, j: (0, 0, i))
    return pl.pallas_call(
        _peer_main_kernel,
        grid=(T_ALL // tt, PEER_EXPERTS // eb),
        in_specs=[pl.BlockSpec((D_MODEL, tt), lambda i, j: (0, i)),
                  pl.BlockSpec((eb, D_MODEL), lambda i, j: (j, 0)),
                  pl.BlockSpec((eb, D_MODEL), lambda i, j: (j, 0)),
                  blk1, blk1, blk3, blk3,
                  pl.BlockSpec((PEER_HEADS, tt), lambda i, j: (0, i))],
        out_specs=pl.BlockSpec((tt, D_MODEL), lambda i, j: (i, 0)),
        out_shape=jax.ShapeDtypeStruct((T_ALL, D_MODEL), F32),
        scratch_shapes=[pltpu.VMEM((eb, tt), F32), pltpu.VMEM((eb, tt), BF16)],
        compiler_params=_params("parallel", "arbitrary"),
    )(ht, u, v, s1, e1, s2, e2, thr)


def kernel(x_prompt, x_sample, state_conv, state_ssm, w_in, conv_w, conv_b, dt_bias, a_log, d_skip,
           ssd_norm_w, sgu_ln_g, sgu_ln_b, sgu_w, sgu_b, w_branch_a, w_branch_b, w_out, ln1_g, ln1_b,
           peer_wq, peer_keys, peer_u, peer_v, ln2_g, ln2_b):
    row = lambda p: p[0].reshape(1, -1)
    x = jnp.concatenate([x_prompt.reshape(T_PROMPT, D_MODEL), x_sample.reshape(DEC_BATCH, D_MODEL)], axis=0)
    xb = x.astype(BF16)

    w = w_in[0]
    o1, o2, o3, o4 = D_INNER, D_INNER + CONV_DIM, D_INNER + CONV_DIM + HEADS, D_INNER + CONV_DIM + HEADS + 2 * D_GATE
    w_dt = jnp.pad(w[:, o2:o3], ((0, 0), (0, DT_PAD - HEADS)))
    mm = functools.partial(_matmul, tm=TOKEN_TILE)
    zs = mm(xb, w[:, :o1].astype(BF16), tn=MM_TN, act=_silu)
    xbc = mm(xb, w[:, o1:o2].astype(BF16), tn=MM_TN)
    dtr = mm(xb, w_dt.astype(BF16), tn=DT_PAD)
    uvg = mm(xb, w[:, o3:o4].astype(BF16), tn=MM_TN, act=_gelu)
    gates = mm(xb, w[:, o4:].astype(BF16), tn=MM_TN, act=jax.nn.sigmoid)

    pad_h = lambda p: jnp.pad(p[0], (0, DT_PAD - HEADS)).reshape(1, DT_PAD)
    dtb, alog = pad_h(dt_bias), pad_h(a_log)
    dexp = jnp.repeat(d_skip[0], HEADDIM).reshape(1, D_INNER)
    convw, convb, normw = conv_w[0], row(conv_b), row(ssd_norm_w)
    lng, lnb = row(sgu_ln_g), row(sgu_ln_b)

    ya, conv_p, ssm_p = _ssd_prompt(xbc, dtr, zs, convw, convb, dtb, alog, dexp, normw)
    bfull = jnp.repeat(sgu_b[0].T, CHUNK, axis=1)
    yb, v_p = _sgu_prompt(uvg, lng, lnb, sgu_w[0], bfull)

    sc_t = jnp.transpose(state_conv[0], (1, 0, 2))
    conv_s_t, xs_s, xdt_s, bm_s, cm_s, dec_s = _sample_pre(xbc, sc_t, dtr, convw, convb, dtb, alog)
    ssm_s, yt_s = _sample_state(dec_s, xdt_s.T, bm_s, cm_s,
                                state_ssm[0].reshape(DEC_BATCH, D_INNER, STATE))
    w0 = jnp.repeat(sgu_w[0][:, 0, 0], CHUNK).reshape(1, D_GATE)
    b0 = jnp.repeat(sgu_b[0][:, 0], CHUNK).reshape(1, D_GATE)
    ya, yb, v_s = _sample_post(yt_s.T, xs_s, zs, dexp, normw, uvg, lng, lnb, w0, b0, ya, yb)

    merged = _merge(ya, w_branch_a[0].astype(BF16), yb, w_branch_b[0].astype(BF16), gates)
    h, hb = _out_ln(merged, w_out[0].astype(BF16), x, row(ln1_g), row(ln1_b))

    ht = hb.T
    qt = _matmul(peer_wq[0].T.astype(BF16), ht, tm=512, tn=TOKEN_TILE, out_dtype=BF16)
    keys = peer_keys[0].reshape(2 * PEER_HEADS, PEER_KEYS, PEER_KEYS).astype(BF16)
    s1, e1, s2, e2, thr = _peer_front(qt, keys)
    p = _peer_main(ht, peer_u[0].astype(BF16), peer_v[0].astype(BF16),
                   jnp.transpose(s1, (1, 0, 2)), jnp.transpose(e1, (1, 0, 2)), s2, e2, thr)
    y = _final_ln(h, p, row(ln2_g), row(ln2_b))

    y_prompt = y[:T_PROMPT].reshape(BATCH, SEQ, D_MODEL)
    y_sample = y[T_PROMPT:].reshape(DEC_BATCH, 1, D_MODEL)
    conv_sample = jnp.transpose(conv_s_t, (1, 0, 2))[None]
    return (y_prompt, y_sample, conv_p[None], ssm_p[None], v_p[None],
            conv_sample, ssm_s.reshape(1, DEC_BATCH, HEADS, HEADDIM, STATE), v_s[None, :, None, :])
```

```python
import functools
import math

import jax
import jax.numpy as jnp
from jax import lax
from jax.experimental import pallas as pl
from jax.experimental.pallas import tpu as pltpu

F32 = jnp.float32
BF16 = jnp.bfloat16

D_MODEL = 2048
SEQ = 2048
BATCH = 4
DEC_BATCH = 128
T_PROMPT = BATCH * SEQ
T_ALL = T_PROMPT + DEC_BATCH
CHUNK = 128
N_CHUNKS = SEQ // CHUNK
SAMPLE_BLOCK = T_PROMPT // CHUNK

D_INNER = 4096
HEADDIM = 64
HEADS = 64
GROUPS = 8
HEADS_PER_GROUP = HEADS // GROUPS
STATE = 128
CONV_K = 4
CONV_DIM = D_INNER + 2 * GROUPS * STATE
B_OFF = D_INNER
C_OFF = D_INNER + GROUPS * STATE
HEAD_PAIRS = HEADS // 2
DT_PAD = 128
D_GATE = 2048
MLP_GROUPS = 16
PEER_HEADS = 8
PEER_KEYS = 128
PEER_EXPERTS = PEER_KEYS * PEER_KEYS
PEER_TOPK = 16
DN_ALPHA = 2.0 ** 0.25
LN_EPS = 1e-5
RMS_EPS = 1e-5
NEG_INF = float("-inf")

TOKEN_TILE = 640
MM_TM = 1040
MM_TN = 1024
SAMPLES_PER_STEP = 2
PEER_EB = 1024
PEER_SUB = 512
PEER_CHUNK = 256
PEER_GATE_ROWS = 64
PEER_TT = 768
T_PAD = 8448
VMEM_LIMIT = 56 * 1024 * 1024


def _params(*sem):
    return pltpu.CompilerParams(dimension_semantics=sem, vmem_limit_bytes=VMEM_LIMIT)


def _gelu(x):
    return 0.5 * x * (1.0 + lax.erf(x * (1.0 / math.sqrt(2.0))))


def _silu(x):
    return x * jax.nn.sigmoid(x)


def _identity(x):
    return x


def _softplus(x):
    return jnp.maximum(x, 0.0) + jnp.log1p(jnp.exp(-jnp.abs(x)))


def _layer_norm(x, g, b):
    mu = jnp.mean(x, axis=-1, keepdims=True)
    xc = x - mu
    var = jnp.mean(xc * xc, axis=-1, keepdims=True)
    return xc * lax.rsqrt(var + LN_EPS) * g + b


def _dot(a, b):
    return jnp.dot(a, b, preferred_element_type=F32)


def _dot_f32(a, b):
    return jnp.dot(a, b, preferred_element_type=F32, precision=lax.Precision.HIGHEST)


def _mm_kernel(a_ref, b_ref, o_ref, *, act):
    o_ref[...] = act(_dot(a_ref[...], b_ref[...])).astype(o_ref.dtype)


def _matmul(a, b, *, tm, tn, name, act=_identity, out_dtype=F32):
    m, k = a.shape
    n = b.shape[1]
    return pl.pallas_call(
        functools.partial(_mm_kernel, act=act),
        grid=(m // tm, n // tn),
        in_specs=[pl.BlockSpec((tm, k), lambda i, j: (i, 0)),
                  pl.BlockSpec((k, tn), lambda i, j: (0, j))],
        out_specs=pl.BlockSpec((tm, tn), lambda i, j: (i, j)),
        out_shape=jax.ShapeDtypeStruct((m, n), out_dtype),
        compiler_params=_params("parallel", "parallel"),
        name=name,
    )(a, b)


def _ssd_gate_norm(y, xs, zs, dexp, normw):
    yg = (y + dexp * xs) * zs
    gw = D_INNER // GROUPS
    outs = []
    for g in range(GROUPS):
        blk = yg[:, g * gw:(g + 1) * gw]
        ms = jnp.mean(blk * blk, axis=-1, keepdims=True)
        outs.append(blk * lax.rsqrt(ms + RMS_EPS) * normw[:, g * gw:(g + 1) * gw])
    return outs


def _ssd_prompt_kernel(xbc_ref, dtr_ref, zs_ref, convw_ref, convb_ref, dtb_ref, alog_ref,
                       dexp_ref, normw_ref,
                       y_ref, convst_ref, ssm_ref,
                       xext, xc_s, rexp_s, rcol_s, colb_s, acumt_s, xdtb_s, xdte_s, stt_s, ysc_s):
    c = pl.program_id(1)

    @pl.when(c == 0)
    def _():
        xext[0:8, :] = jnp.zeros((8, CONV_DIM), F32)
        stt_s[...] = jnp.zeros(stt_s.shape, F32)
        hrow = lax.broadcasted_iota(jnp.int32, (DT_PAD, D_INNER), 0)
        col = lax.broadcasted_iota(jnp.int32, (DT_PAD, D_INNER), 1)
        rexp_s[...] = jnp.where(lax.shift_right_logical(col, 6) == hrow, 1.0, 0.0).astype(F32)
        hrow = lax.broadcasted_iota(jnp.int32, (DT_PAD, HEADS * CHUNK), 0)
        col = lax.broadcasted_iota(jnp.int32, (DT_PAD, HEADS * CHUNK), 1)
        rcol_s[...] = jnp.where(lax.shift_right_logical(col, 7) == hrow, 1.0, 0.0).astype(F32)

    xext[8:8 + CHUNK, :] = xbc_ref[...]
    slab = 512
    for s0 in range(0, CONV_DIM, slab):
        sl = slice(s0, s0 + slab)
        acc = convb_ref[:, sl] + convw_ref[3:4, sl] * xext[8:8 + CHUNK, sl]
        acc = acc + convw_ref[2:3, sl] * xext[7:7 + CHUNK, sl]
        acc = acc + convw_ref[1:2, sl] * xext[6:6 + CHUNK, sl]
        acc = acc + convw_ref[0:1, sl] * xext[5:5 + CHUNK, sl]
        xc_s[:, sl] = _silu(acc)
    convst_ref[0] = xbc_ref[CHUNK - 3:CHUNK, :]
    xext[0:8, :] = xbc_ref[CHUNK - 8:CHUNK, :]

    dtv = _softplus(dtr_ref[...] + dtb_ref[...])
    a = dtv * (-jnp.exp(alog_ref[...]))
    row = lax.broadcasted_iota(jnp.int32, (CHUNK, CHUNK), 0)
    colq = lax.broadcasted_iota(jnp.int32, (CHUNK, CHUNK), 1)
    causal = row >= colq
    acum = _dot_f32(jnp.where(causal, 1.0, 0.0).astype(F32), a)
    acumt_s[...] = acum.T
    colb_s[...] = _dot_f32(acum, rcol_s[...])
    dt_exp = _dot_f32(dtv, rexp_s[...])
    acum_exp = _dot_f32(acum, rexp_s[...])
    last_exp = acum_exp[CHUNK - 1:CHUNK, :]
    xdt = xc_s[:, 0:D_INNER] * dt_exp
    xdtb_s[...] = xdt.astype(BF16)
    xdte_s[...] = (xdt * jnp.exp(last_exp - acum_exp)).astype(BF16)
    chunk_decay = jnp.exp(last_exp)

    lane = lax.broadcasted_iota(jnp.int32, (CHUNK, 128), 1)
    lo = lane < HEADDIM
    zero_b = jnp.zeros((CHUNK, 128), BF16)
    for g in range(GROUPS):
        bg = xc_s[:, B_OFF + g * STATE:B_OFF + (g + 1) * STATE]
        cg = xc_s[:, C_OFF + g * STATE:C_OFF + (g + 1) * STATE]
        bgb = bg.astype(BF16)
        cb = lax.dot_general(cg.astype(BF16), bgb, (((1,), (1,)), ((), ())),
                             preferred_element_type=F32)
        for pr in range(HEADS_PER_GROUP // 2):
            hp = g * (HEADS_PER_GROUP // 2) + pr
            psl = slice(hp * 128, (hp + 1) * 128)
            lhs = []
            for hh in (2 * hp, 2 * hp + 1):
                colb = colb_s[:, hh * CHUNK:(hh + 1) * CHUNK]
                seg = colb - acumt_s[hh:hh + 1, :]
                decay = jnp.exp(jnp.where(causal, seg, NEG_INF))
                lhs.append((cb * decay).astype(BF16))
            for hh in (2 * hp, 2 * hp + 1):
                colb = colb_s[:, hh * CHUNK:(hh + 1) * CHUNK]
                lhs.append((cg * jnp.exp(colb)).astype(BF16))
            xpair = xdtb_s[:, psl]
            spair = stt_s[hp].astype(BF16)
            rhs = jnp.concatenate([jnp.where(lo, xpair, zero_b), jnp.where(lo, zero_b, xpair),
                                   jnp.where(lo, spair, zero_b), jnp.where(lo, zero_b, spair)],
                                  axis=0)
            ysc_s[:, psl] = _dot(jnp.concatenate(lhs, axis=1), rhs)
            upd = lax.dot_general(bgb, xdte_s[:, psl], (((0,), (0,)), ((), ())),
                                  preferred_element_type=F32)
            stt_s[hp] = stt_s[hp] * chunk_decay[:, psl] + upd

    outs = _ssd_gate_norm(ysc_s[...], xc_s[:, 0:D_INNER], zs_ref[...], dexp_ref[...], normw_ref[...])
    gw = D_INNER // GROUPS
    for g in range(GROUPS):
        y_ref[:, g * gw:(g + 1) * gw] = outs[g].astype(y_ref.dtype)

    @pl.when(c == N_CHUNKS - 1)
    def _():
        for hp in range(HEAD_PAIRS):
            st = stt_s[hp].T
            ssm_ref[0, 2 * hp] = st[0:HEADDIM, :]
            ssm_ref[0, 2 * hp + 1] = st[HEADDIM:2 * HEADDIM, :]


def _ssd_prompt(xbc, dtr, zs, convw, convb, dtb, alog, dexp, normw):
    rowblk = lambda b, c: (b * N_CHUNKS + c, 0)
    const = lambda b, c: (0, 0)
    return pl.pallas_call(
        _ssd_prompt_kernel,
        grid=(BATCH, N_CHUNKS),
        in_specs=[pl.BlockSpec((CHUNK, CONV_DIM), rowblk),
                  pl.BlockSpec((CHUNK, DT_PAD), rowblk),
                  pl.BlockSpec((CHUNK, D_INNER), rowblk),
                  pl.BlockSpec((CONV_K, CONV_DIM), const),
                  pl.BlockSpec((1, CONV_DIM), const),
                  pl.BlockSpec((1, DT_PAD), const),
                  pl.BlockSpec((1, DT_PAD), const),
                  pl.BlockSpec((1, D_INNER), const),
                  pl.BlockSpec((1, D_INNER), const)],
        out_specs=[pl.BlockSpec((CHUNK, D_INNER), rowblk),
                   pl.BlockSpec((1, CONV_K - 1, CONV_DIM), lambda b, c: (b, 0, 0)),
                   pl.BlockSpec((1, HEADS, HEADDIM, STATE), lambda b, c: (b, 0, 0, 0))],
        out_shape=[jax.ShapeDtypeStruct((T_ALL, D_INNER), BF16),
                   jax.ShapeDtypeStruct((BATCH, CONV_K - 1, CONV_DIM), F32),
                   jax.ShapeDtypeStruct((BATCH, HEADS, HEADDIM, STATE), F32)],
        scratch_shapes=[pltpu.VMEM((8 + CHUNK, CONV_DIM), F32),
                        pltpu.VMEM((CHUNK, CONV_DIM), F32),
                        pltpu.VMEM((DT_PAD, D_INNER), F32),
                        pltpu.VMEM((DT_PAD, HEADS * CHUNK), F32),
                        pltpu.VMEM((CHUNK, HEADS * CHUNK), F32),
                        pltpu.VMEM((DT_PAD, CHUNK), F32),
                        pltpu.VMEM((CHUNK, D_INNER), BF16),
                        pltpu.VMEM((CHUNK, D_INNER), BF16),
                        pltpu.VMEM((HEAD_PAIRS, STATE, 128), F32),
                        pltpu.VMEM((CHUNK, D_INNER), F32)],
        compiler_params=_params("arbitrary", "arbitrary"),
        name="ssd_prompt",
    )(xbc, dtr, zs, convw, convb, dtb, alog, dexp, normw)


def _sgu_prompt_kernel(uvg_ref, lng_ref, lnb_ref, w_ref, bfull_ref, yb_ref, v_ref):
    vn = _layer_norm(uvg_ref[:, D_GATE:2 * D_GATE], lng_ref[...], lnb_ref[...])
    v_ref[0] = vn
    row = lax.broadcasted_iota(jnp.int32, (CHUNK, CHUNK), 0)
    col = lax.broadcasted_iota(jnp.int32, (CHUNK, CHUNK), 1)
    causal = row >= col
    for g in range(MLP_GROUPS):
        sl = slice(g * 128, (g + 1) * 128)
        wg = jnp.where(causal, w_ref[g], 0.0).astype(BF16)
        s = _dot(wg, vn[:, sl].astype(BF16)) + bfull_ref[:, sl]
        yb_ref[:, sl] = (uvg_ref[:, sl] * s).astype(yb_ref.dtype)


def _sgu_prompt(uvg, lng, lnb, w, bfull):
    const2 = lambda i: (0, 0)
    return pl.pallas_call(
        _sgu_prompt_kernel,
        grid=(BATCH * N_CHUNKS,),
        in_specs=[pl.BlockSpec((CHUNK, 2 * D_GATE), lambda i: (i, 0)),
                  pl.BlockSpec((1, D_GATE), const2),
                  pl.BlockSpec((1, D_GATE), const2),
                  pl.BlockSpec((MLP_GROUPS, CHUNK, CHUNK), lambda i: (0, 0, 0)),
                  pl.BlockSpec((CHUNK, D_GATE), const2)],
        out_specs=[pl.BlockSpec((CHUNK, D_GATE), lambda i: (i, 0)),
                   pl.BlockSpec((1, CHUNK, D_GATE), lambda i: (i // N_CHUNKS, 0, 0))],
        out_shape=[jax.ShapeDtypeStruct((T_ALL, D_GATE), BF16),
                   jax.ShapeDtypeStruct((BATCH, CHUNK, D_GATE), F32)],
        compiler_params=_params("arbitrary"),
        name="sgu_prompt",
    )(uvg, lng, lnb, w, bfull)


def _sample_pre_kernel(xbc_ref, sc_ref, dtr_ref, convw_ref, convb_ref, dtb_ref, alog_ref,
                       convnew_ref, xs_ref, xdt_ref, bm_ref, cm_ref, dec_ref):
    x = xbc_ref[...]
    acc = convb_ref[...] + convw_ref[3:4, :] * x
    acc = acc + convw_ref[2:3, :] * sc_ref[2]
    acc = acc + convw_ref[1:2, :] * sc_ref[1]
    acc = acc + convw_ref[0:1, :] * sc_ref[0]
    xc = _silu(acc)
    convnew_ref[0] = sc_ref[1]
    convnew_ref[1] = sc_ref[2]
    convnew_ref[2] = x
    dtv = _softplus(dtr_ref[...] + dtb_ref[...])
    dec_ref[...] = jnp.exp(dtv * (-jnp.exp(alog_ref[...])))
    hrow = lax.broadcasted_iota(jnp.int32, (DT_PAD, D_INNER), 0)
    col = lax.broadcasted_iota(jnp.int32, (DT_PAD, D_INNER), 1)
    rexp = jnp.where(lax.shift_right_logical(col, 6) == hrow, 1.0, 0.0).astype(F32)
    xs = xc[:, 0:D_INNER]
    xs_ref[...] = xs
    xdt_ref[...] = xs * _dot_f32(dtv, rexp)
    bm_ref[...] = xc[:, B_OFF:C_OFF]
    cm_ref[...] = xc[:, C_OFF:CONV_DIM]


def _sample_pre(xbc, sc_t, dtr, convw, convb, dtb, alog):
    blk = lambda i: (SAMPLE_BLOCK, 0)
    const = lambda i: (0, 0)
    return pl.pallas_call(
        _sample_pre_kernel,
        grid=(1,),
        in_specs=[pl.BlockSpec((DEC_BATCH, CONV_DIM), blk),
                  pl.BlockSpec((CONV_K - 1, DEC_BATCH, CONV_DIM), lambda i: (0, 0, 0)),
                  pl.BlockSpec((DEC_BATCH, DT_PAD), blk),
                  pl.BlockSpec((CONV_K, CONV_DIM), const),
                  pl.BlockSpec((1, CONV_DIM), const),
                  pl.BlockSpec((1, DT_PAD), const),
                  pl.BlockSpec((1, DT_PAD), const)],
        out_specs=[pl.BlockSpec((CONV_K - 1, DEC_BATCH, CONV_DIM), lambda i: (0, 0, 0)),
                   pl.BlockSpec((DEC_BATCH, D_INNER), const),
                   pl.BlockSpec((DEC_BATCH, D_INNER), const),
                   pl.BlockSpec((DEC_BATCH, GROUPS * STATE), const),
                   pl.BlockSpec((DEC_BATCH, GROUPS * STATE), const),
                   pl.BlockSpec((DEC_BATCH, DT_PAD), const)],
        out_shape=[jax.ShapeDtypeStruct((CONV_K - 1, DEC_BATCH, CONV_DIM), F32),
                   jax.ShapeDtypeStruct((DEC_BATCH, D_INNER), F32),
                   jax.ShapeDtypeStruct((DEC_BATCH, D_INNER), F32),
                   jax.ShapeDtypeStruct((DEC_BATCH, GROUPS * STATE), F32),
                   jax.ShapeDtypeStruct((DEC_BATCH, GROUPS * STATE), F32),
                   jax.ShapeDtypeStruct((DEC_BATCH, DT_PAD), F32)],
        compiler_params=_params("arbitrary"),
        name="sample_pre",
    )(xbc, sc_t, dtr, convw, convb, dtb, alog)


def _sample_state_kernel(dec_ref, xdtt_ref, bm_ref, cmt_ref, st_ref, new_ref, yt_ref):
    srow = lax.broadcasted_iota(jnp.int32, (DEC_BATCH, STATE), 0)
    slane = lax.broadcasted_iota(jnp.int32, (STATE, DEC_BATCH), 1)
    gw = HEADS_PER_GROUP * HEADDIM

    @pl.when(pl.program_id(0) == 0)
    def _():
        yt_ref[...] = jnp.zeros(yt_ref.shape, F32)

    for i in range(SAMPLES_PER_STEP):
        s = pl.program_id(0) * SAMPLES_PER_STEP + i
        brow = bm_ref[pl.ds(s, 1), :]
        for g in range(GROUPS):
            sel_b = jnp.where(srow == s, brow[:, g * STATE:(g + 1) * STATE], 0.0).astype(BF16)
            upd = _dot(xdtt_ref[g * gw:(g + 1) * gw, :], sel_b)
            news = []
            for r in range(HEADS_PER_GROUP):
                h = g * HEADS_PER_GROUP + r
                rows = slice(h * HEADDIM, (h + 1) * HEADDIM)
                new = st_ref[i, rows, :] * dec_ref[s, h] + upd[r * HEADDIM:(r + 1) * HEADDIM, :]
                new_ref[i, rows, :] = new
                news.append(new.astype(BF16))
            sel_c = jnp.where(slane == s, cmt_ref[g * STATE:(g + 1) * STATE, :], 0.0).astype(BF16)
            yt_ref[g * gw:(g + 1) * gw, :] += _dot(jnp.concatenate(news, axis=0), sel_c)


def _sample_state(dec, xdtt, bm, cmt, state):
    const = lambda s: (0, 0)
    blk = pl.BlockSpec((SAMPLES_PER_STEP, D_INNER, STATE), lambda s: (s, 0, 0))
    return pl.pallas_call(
        _sample_state_kernel,
        grid=(DEC_BATCH // SAMPLES_PER_STEP,),
        in_specs=[pl.BlockSpec(memory_space=pltpu.SMEM),
                  pl.BlockSpec((D_INNER, DEC_BATCH), const),
                  pl.BlockSpec((DEC_BATCH, GROUPS * STATE), const),
                  pl.BlockSpec((GROUPS * STATE, DEC_BATCH), const),
                  blk],
        out_specs=[blk, pl.BlockSpec((D_INNER, DEC_BATCH), const)],
        out_shape=[jax.ShapeDtypeStruct((DEC_BATCH, D_INNER, STATE), F32),
                   jax.ShapeDtypeStruct((D_INNER, DEC_BATCH), F32)],
        compiler_params=_params("arbitrary"),
        name="sample_state",
    )(dec, xdtt, bm, cmt, state)


def _sample_post_kernel(y_ref, xs_ref, zs_ref, dexp_ref, normw_ref, uvg_ref, lng_ref, lnb_ref,
                        w0_ref, b0_ref, ya_in, yb_in, ya_ref, yb_ref, v_ref):
    del ya_in, yb_in
    outs = _ssd_gate_norm(y_ref[...], xs_ref[...], zs_ref[...], dexp_ref[...], normw_ref[...])
    gw = D_INNER // GROUPS
    for g in range(GROUPS):
        ya_ref[:, g * gw:(g + 1) * gw] = outs[g].astype(ya_ref.dtype)
    vn = _layer_norm(uvg_ref[:, D_GATE:2 * D_GATE], lng_ref[...], lnb_ref[...])
    v_ref[...] = vn
    s = w0_ref[...] * vn + b0_ref[...]
    yb_ref[...] = (uvg_ref[:, 0:D_GATE] * s).astype(yb_ref.dtype)


def _sample_post(y, xs, zs, dexp, normw, uvg, lng, lnb, w0, b0, ya, yb):
    blk = lambda i: (SAMPLE_BLOCK, 0)
    const = lambda i: (0, 0)
    return pl.pallas_call(
        _sample_post_kernel,
        grid=(1,),
        in_specs=[pl.BlockSpec((DEC_BATCH, D_INNER), const),
                  pl.BlockSpec((DEC_BATCH, D_INNER), const),
                  pl.BlockSpec((DEC_BATCH, D_INNER), blk),
                  pl.BlockSpec((1, D_INNER), const),
                  pl.BlockSpec((1, D_INNER), const),
                  pl.BlockSpec((DEC_BATCH, 2 * D_GATE), blk),
                  pl.BlockSpec((1, D_GATE), const),
                  pl.BlockSpec((1, D_GATE), const),
                  pl.BlockSpec((1, D_GATE), const),
                  pl.BlockSpec((1, D_GATE), const),
                  pl.BlockSpec(memory_space=pl.ANY),
                  pl.BlockSpec(memory_space=pl.ANY)],
        out_specs=[pl.BlockSpec((DEC_BATCH, D_INNER), blk),
                   pl.BlockSpec((DEC_BATCH, D_GATE), blk),
                   pl.BlockSpec((DEC_BATCH, D_GATE), const)],
        out_shape=[jax.ShapeDtypeStruct((T_ALL, D_INNER), BF16),
                   jax.ShapeDtypeStruct((T_ALL, D_GATE), BF16),
                   jax.ShapeDtypeStruct((DEC_BATCH, D_GATE), F32)],
        input_output_aliases={10: 0, 11: 1},
        compiler_params=_params("arbitrary"),
        name="sample_post",
    )(y, xs, zs, dexp, normw, uvg, lng, lnb, w0, b0, ya, yb)


def _merge_kernel(ya_ref, wa_ref, yb_ref, wb_ref, ga_ref, gb_ref, o_ref):
    o_ref[...] = (ga_ref[...] * _dot(ya_ref[...], wa_ref[...])
                  + gb_ref[...] * _dot(yb_ref[...], wb_ref[...])).astype(o_ref.dtype)


def _merge(ya, wa, yb, wb, gates):
    tm, tn = MM_TM, MM_TN // 2
    nb = D_MODEL // tn
    return pl.pallas_call(
        _merge_kernel,
        grid=(T_ALL // tm, nb),
        in_specs=[pl.BlockSpec((tm, D_INNER), lambda i, j: (i, 0)),
                  pl.BlockSpec((D_INNER, tn), lambda i, j: (0, j)),
                  pl.BlockSpec((tm, D_GATE), lambda i, j: (i, 0)),
                  pl.BlockSpec((D_GATE, tn), lambda i, j: (0, j)),
                  pl.BlockSpec((tm, tn), lambda i, j: (i, j)),
                  pl.BlockSpec((tm, tn), lambda i, j: (i, j + nb))],
        out_specs=pl.BlockSpec((tm, tn), lambda i, j: (i, j)),
        out_shape=jax.ShapeDtypeStruct((T_ALL, D_MODEL), BF16),
        compiler_params=_params("parallel", "parallel"),
        name="merge",
    )(ya, wa, yb, wb, gates, gates)


def _out_ln_kernel(m_ref, w_ref, x_ref, g_ref, b_ref, h_ref, hb_ref):
    h = _layer_norm(DN_ALPHA * x_ref[...] + _dot(m_ref[...], w_ref[...]), g_ref[...], b_ref[...])
    h_ref[...] = h
    hb_ref[...] = h.astype(BF16)


def _out_ln(merged, wout, x, g, b):
    tm = TOKEN_TILE
    const = lambda i: (0, 0)
    rows = lambda i: (i, 0)
    return pl.pallas_call(
        _out_ln_kernel,
        grid=(T_ALL // tm,),
        in_specs=[pl.BlockSpec((tm, D_MODEL), rows),
                  pl.BlockSpec((D_MODEL, D_MODEL), const),
                  pl.BlockSpec((tm, D_MODEL), rows),
                  pl.BlockSpec((1, D_MODEL), const),
                  pl.BlockSpec((1, D_MODEL), const)],
        out_specs=[pl.BlockSpec((tm, D_MODEL), rows), pl.BlockSpec((tm, D_MODEL), rows)],
        out_shape=[jax.ShapeDtypeStruct((T_ALL, D_MODEL), F32),
                   jax.ShapeDtypeStruct((T_ALL, D_MODEL), BF16)],
        compiler_params=_params("parallel"),
        name="out_ln",
    )(merged, wout, x, g, b)


def _final_ln_kernel(h_ref, p_ref, g_ref, b_ref, o_ref):
    o_ref[...] = _layer_norm(DN_ALPHA * h_ref[...] + p_ref[...], g_ref[...], b_ref[...])


def _final_ln(h, p, g, b):
    tm = TOKEN_TILE
    const = lambda i: (0, 0)
    rows = lambda i: (i, 0)
    return pl.pallas_call(
        _final_ln_kernel,
        grid=(T_ALL // tm,),
        in_specs=[pl.BlockSpec((tm, D_MODEL), rows), pl.BlockSpec((tm, D_MODEL), rows),
                  pl.BlockSpec((1, D_MODEL), const), pl.BlockSpec((1, D_MODEL), const)],
        out_specs=pl.BlockSpec((tm, D_MODEL), rows),
        out_shape=jax.ShapeDtypeStruct((T_ALL, D_MODEL), F32),
        compiler_params=_params("parallel"),
        name="final_ln",
    )(h, p, g, b)


def _peer_front_kernel(qt_ref, keys_ref, s1_ref, e1_ref, s2_ref, e2_ref, thr_ref):
    k = PEER_TOPK
    for h in range(PEER_HEADS):
        sc = []
        top = []
        for side in range(2):
            hs = 2 * h + side
            x = _dot(keys_ref[hs], qt_ref[hs * PEER_KEYS:(hs + 1) * PEER_KEYS, :])
            sc.append(x)
            vals = []
            for _ in range(k):
                m = jnp.max(x, axis=0, keepdims=True)
                vals.append(m)
                x = jnp.where(x == m, NEG_INF, x)
            top.append(vals)
        cands = [top[0][i] + top[1][j] for i in range(k) for j in range(k) if (i + 1) * (j + 1) <= k]
        pad = (-len(cands)) % 8
        tt = cands[0].shape[1]
        cand = jnp.concatenate(cands + [jnp.full((pad, tt), NEG_INF, F32)], axis=0)
        idx = lax.broadcasted_iota(jnp.int32, cand.shape, 0)
        best = []
        for _ in range(k):
            m = jnp.max(cand, axis=0, keepdims=True)
            best.append(m)
            first = jnp.min(jnp.where(cand == m, idx, cand.shape[0]), axis=0, keepdims=True)
            cand = jnp.where(idx == first, NEG_INF, cand)
        z = jnp.zeros_like(best[0])
        for bk in best:
            z = z + jnp.exp(bk - best[0])
        s1_ref[h] = sc[0]
        s2_ref[h] = sc[1]
        e1_ref[h] = jnp.exp(sc[0] - top[0][0]) * (1.0 / z)
        e2_ref[h] = jnp.exp(sc[1] - top[1][0])
        thr_ref[h:h + 1, :] = best[k - 1]


def _peer_front(qt, keys):
    tt = 128
    blk3 = pl.BlockSpec((PEER_HEADS, PEER_KEYS, tt), lambda i: (0, 0, i))
    shp3 = jax.ShapeDtypeStruct((PEER_HEADS, PEER_KEYS, T_PAD), F32)
    return pl.pallas_call(
        _peer_front_kernel,
        grid=(T_PAD // tt,),
        in_specs=[pl.BlockSpec((2 * PEER_HEADS * PEER_KEYS, tt), lambda i: (0, i)),
                  pl.BlockSpec((2 * PEER_HEADS, PEER_KEYS, PEER_KEYS), lambda i: (0, 0, 0))],
        out_specs=[blk3, blk3, blk3, blk3, pl.BlockSpec((PEER_HEADS, tt), lambda i: (0, i))],
        out_shape=[shp3, shp3, shp3, shp3, jax.ShapeDtypeStruct((PEER_HEADS, T_PAD), F32)],
        compiler_params=_params("parallel"),
        name="peer_front",
    )(qt, keys)


def _peer_main_kernel(ht_ref, u_ref, v_ref, s1_ref, e1_ref, s2_ref, e2_ref, thr_ref, o_ref):
    j = pl.program_id(1)
    tt = ht_ref.shape[1]
    n_i1 = PEER_SUB // PEER_KEYS
    hk = PEER_GATE_ROWS
    n_chunks = tt // PEER_CHUNK
    chains = [(sub, c) for sub in range(PEER_EB // PEER_SUB) for c in range(n_chunks)]
    i1_rows = pl.ds(pl.multiple_of(j * (PEER_EB // PEER_KEYS), PEER_EB // PEER_KEYS), PEER_EB // PEER_KEYS)

    @pl.when(j == 0)
    def _():
        o_ref[...] = jnp.zeros(o_ref.shape, F32)

    def activations(k):
        sub, c = chains[k]
        return _dot(u_ref[sub * PEER_SUB:(sub + 1) * PEER_SUB, :],
                    ht_ref[:, c * PEER_CHUNK:(c + 1) * PEER_CHUNK])

    def gating(k, act):
        sub, c = chains[k]
        tok_tiles = []
        for tc in range(PEER_CHUNK // 128):
            cols = slice(c * PEER_CHUNK + tc * 128, c * PEER_CHUNK + (tc + 1) * 128)
            lcols = slice(tc * 128, (tc + 1) * 128)
            parts = []
            for part in range(PEER_KEYS // hk):
                krows = slice(part * hk, (part + 1) * hk)
                gates = [jnp.zeros((hk, 128), F32) for _ in range(n_i1)]
                for h in range(PEER_HEADS):
                    s2t = s2_ref[h, krows, cols]
                    e2t = e2_ref[h, krows, cols]
                    thr = thr_ref[h:h + 1, cols]
                    s1t = s1_ref[h, i1_rows, cols]
                    e1t = e1_ref[h, i1_rows, cols]
                    for r in range(n_i1):
                        q = sub * n_i1 + r
                        pair = s1t[q:q + 1, :] + s2t
                        w = e1t[q:q + 1, :] * e2t
                        gates[r] = gates[r] + jnp.where(pair >= thr, w, 0.0)
                for r in range(n_i1):
                    arows = slice(r * PEER_KEYS + part * hk, r * PEER_KEYS + (part + 1) * hk)
                    gates[r] = gates[r] * _gelu(act[arows, lcols])
                parts.append(gates)
            tiles = [jnp.concatenate([p[r] for p in parts], axis=0).T.astype(BF16)
                     for r in range(n_i1)]
            tok_tiles.append(jnp.concatenate(tiles, axis=1))
        return jnp.concatenate(tok_tiles, axis=0)

    def accumulate(k, gt):
        sub, c = chains[k]
        o_ref[c * PEER_CHUNK:(c + 1) * PEER_CHUNK, :] += _dot(
            gt, v_ref[sub * PEER_SUB:(sub + 1) * PEER_SUB, :])

    act = activations(0)
    for k in range(len(chains)):
        act_next = activations(k + 1) if k + 1 < len(chains) else None
        accumulate(k, gating(k, act))
        act = act_next


def _peer_main(ht, u, v, s1, e1, s2, e2, thr):
    tt, eb = PEER_TT, PEER_EB
    once = pl.Buffered(1)
    blk3 = pl.BlockSpec((PEER_HEADS, PEER_KEYS, tt), lambda i, j: (0, 0, i), pipeline_mode=once)
    return pl.pallas_call(
        _peer_main_kernel,
        grid=(T_PAD // tt, PEER_EXPERTS // eb),
        in_specs=[pl.BlockSpec((D_MODEL, tt), lambda i, j: (0, i), pipeline_mode=once),
                  pl.BlockSpec((eb, D_MODEL), lambda i, j: (j, 0)),
                  pl.BlockSpec((eb, D_MODEL), lambda i, j: (j, 0)),
                  blk3, blk3, blk3, blk3,
                  pl.BlockSpec((PEER_HEADS, tt), lambda i, j: (0, i), pipeline_mode=once)],
        out_specs=pl.BlockSpec((tt, D_MODEL), lambda i, j: (i, 0)),
        out_shape=jax.ShapeDtypeStruct((T_PAD, D_MODEL), F32),
        compiler_params=_params("parallel", "arbitrary"),
        name="peer_main",
    )(ht, u, v, s1, e1, s2, e2, thr)


def kernel(x_prompt, x_sample, state_conv, state_ssm, w_in, conv_w, conv_b, dt_bias, a_log, d_skip,
           ssd_norm_w, sgu_ln_g, sgu_ln_b, sgu_w, sgu_b, w_branch_a, w_branch_b, w_out, ln1_g, ln1_b,
           peer_wq, peer_keys, peer_u, peer_v, ln2_g, ln2_b):
    row = lambda p: p[0].reshape(1, -1)
    x = jnp.concatenate([x_prompt.reshape(T_PROMPT, D_MODEL), x_sample.reshape(DEC_BATCH, D_MODEL)], axis=0)
    xb = x.astype(BF16)

    w = w_in[0]
    o1, o2, o3, o4 = D_INNER, D_INNER + CONV_DIM, D_INNER + CONV_DIM + HEADS, D_INNER + CONV_DIM + HEADS + 2 * D_GATE
    w_dt = jnp.pad(w[:, o2:o3], ((0, 0), (0, DT_PAD - HEADS)))
    mm = functools.partial(_matmul, tm=MM_TM)
    zs = mm(xb, w[:, :o1].astype(BF16), tn=MM_TN, act=_silu, name="proj_z")
    xbc = mm(xb, w[:, o1:o2].astype(BF16), tn=MM_TN, name="proj_xbc")
    dtr = mm(xb, w_dt.astype(BF16), tn=DT_PAD, name="proj_dt")
    uvg = mm(xb, w[:, o3:o4].astype(BF16), tn=MM_TN, act=_gelu, name="proj_uv")
    gates = mm(xb, w[:, o4:].astype(BF16), tn=MM_TN, act=jax.nn.sigmoid, name="proj_gates")

    pad_h = lambda p: jnp.pad(p[0], (0, DT_PAD - HEADS)).reshape(1, DT_PAD)
    dtb, alog = pad_h(dt_bias), pad_h(a_log)
    dexp = jnp.repeat(d_skip[0], HEADDIM).reshape(1, D_INNER)
    convw, convb, normw = conv_w[0], row(conv_b), row(ssd_norm_w)
    lng, lnb = row(sgu_ln_g), row(sgu_ln_b)

    ya, conv_p, ssm_p = _ssd_prompt(xbc, dtr, zs, convw, convb, dtb, alog, dexp, normw)
    bfull = jnp.repeat(sgu_b[0].T, CHUNK, axis=1)
    yb, v_p = _sgu_prompt(uvg, lng, lnb, sgu_w[0], bfull)

    sc_t = jnp.transpose(state_conv[0], (1, 0, 2))
    conv_s_t, xs_s, xdt_s, bm_s, cm_s, dec_s = _sample_pre(xbc, sc_t, dtr, convw, convb, dtb, alog)
    ssm_s, yt_s = _sample_state(dec_s, xdt_s.T.astype(BF16), bm_s, cm_s.T,
                                state_ssm[0].reshape(DEC_BATCH, D_INNER, STATE))
    w0 = jnp.repeat(sgu_w[0][:, 0, 0], CHUNK).reshape(1, D_GATE)
    b0 = jnp.repeat(sgu_b[0][:, 0], CHUNK).reshape(1, D_GATE)
    ya, yb, v_s = _sample_post(yt_s.T, xs_s, zs, dexp, normw, uvg, lng, lnb, w0, b0, ya, yb)

    merged = _merge(ya, w_branch_a[0].astype(BF16), yb, w_branch_b[0].astype(BF16), gates)
    h, hb = _out_ln(merged, w_out[0].astype(BF16), x, row(ln1_g), row(ln1_b))

    ht = jnp.pad(hb.T, ((0, 0), (0, T_PAD - T_ALL)))
    qt = _matmul(peer_wq[0].T.astype(BF16), ht, tm=512, tn=PEER_TT, out_dtype=BF16, name="peer_query")
    keys = peer_keys[0].reshape(2 * PEER_HEADS, PEER_KEYS, PEER_KEYS).astype(BF16)
    s1, e1, s2, e2, thr = _peer_front(qt, keys)
    p = _peer_main(ht, peer_u[0].astype(BF16), peer_v[0].astype(BF16), s1, e1, s2, e2, thr)
    y = _final_ln(h, p, row(ln2_g), row(ln2_b))

    y_prompt = y[:T_PROMPT].reshape(BATCH, SEQ, D_MODEL)
    y_sample = y[T_PROMPT:].reshape(DEC_BATCH, 1, D_MODEL)
    conv_sample = jnp.transpose(conv_s_t, (1, 0, 2))[None]
    return (y_prompt, y_sample, conv_p[None], ssm_p[None], v_p[None],
            conv_sample, ssm_s.reshape(1, DEC_BATCH, HEADS, HEADDIM, STATE), v_s[None, :, None, :])
```

```python
import functools
import math

import jax
import jax.numpy as jnp
from jax import lax
from jax.experimental import pallas as pl
from jax.experimental.pallas import tpu as pltpu

F32 = jnp.float32
BF16 = jnp.bfloat16

D_MODEL = 2048
SEQ = 2048
BATCH = 4
DEC_BATCH = 128
T_PROMPT = BATCH * SEQ
T_ALL = T_PROMPT + DEC_BATCH
CHUNK = 128
N_CHUNKS = SEQ // CHUNK
SAMPLE_BLOCK = T_PROMPT // CHUNK

D_INNER = 4096
HEADDIM = 64
HEADS = 64
GROUPS = 8
HEADS_PER_GROUP = HEADS // GROUPS
STATE = 128
CONV_K = 4
CONV_DIM = D_INNER + 2 * GROUPS * STATE
B_OFF = D_INNER
C_OFF = D_INNER + GROUPS * STATE
HEAD_PAIRS = HEADS // 2
DT_PAD = 128
D_GATE = 2048
MLP_GROUPS = 16
PEER_HEADS = 8
PEER_KEYS = 128
PEER_EXPERTS = PEER_KEYS * PEER_KEYS
PEER_TOPK = 16
DN_ALPHA = 2.0 ** 0.25
LN_EPS = 1e-5
RMS_EPS = 1e-5
NEG_INF = float("-inf")

TOKEN_TILE = 640
MM_TM = 1040
MM_TN = 1024
SAMPLES_PER_STEP = 2
PEER_EB = 1024
PEER_SUB = 512
PEER_CHUNK = 256
PEER_GATE_ROWS = 64
PEER_TT = 768
T_PAD = 8448
VMEM_LIMIT = 56 * 1024 * 1024


def _params(*sem):
    return pltpu.CompilerParams(dimension_semantics=sem, vmem_limit_bytes=VMEM_LIMIT)


def _gelu(x):
    return 0.5 * x * (1.0 + lax.erf(x * (1.0 / math.sqrt(2.0))))


def _silu(x):
    return x * jax.nn.sigmoid(x)


def _identity(x):
    return x


def _softplus(x):
    return jnp.maximum(x, 0.0) + jnp.log1p(jnp.exp(-jnp.abs(x)))


def _layer_norm(x, g, b):
    mu = jnp.mean(x, axis=-1, keepdims=True)
    xc = x - mu
    var = jnp.mean(xc * xc, axis=-1, keepdims=True)
    return xc * lax.rsqrt(var + LN_EPS) * g + b


def _dot(a, b):
    return jnp.dot(a, b, preferred_element_type=F32)


def _dot_f32(a, b):
    return jnp.dot(a, b, preferred_element_type=F32, precision=lax.Precision.HIGHEST)


def _mm_kernel(a_ref, b_ref, o_ref, *, act):
    o_ref[...] = act(_dot(a_ref[...], b_ref[...])).astype(o_ref.dtype)


def _matmul(a, b, *, tm, tn, name, act=_identity, out_dtype=F32):
    m, k = a.shape
    n = b.shape[1]
    return pl.pallas_call(
        functools.partial(_mm_kernel, act=act),
        grid=(m // tm, n // tn),
        in_specs=[pl.BlockSpec((tm, k), lambda i, j: (i, 0)),
                  pl.BlockSpec((k, tn), lambda i, j: (0, j))],
        out_specs=pl.BlockSpec((tm, tn), lambda i, j: (i, j)),
        out_shape=jax.ShapeDtypeStruct((m, n), out_dtype),
        compiler_params=_params("parallel", "parallel"),
        name=name,
    )(a, b)


def _mm_wcast_kernel(a_ref, w_ref, o_ref, wb_s, *, act):
    @pl.when(pl.program_id(1) == 0)
    def _():
        wb_s[...] = w_ref[...].astype(BF16)

    o_ref[...] = act(_dot(a_ref[...], wb_s[...])).astype(o_ref.dtype)


def _matmul_wcast(a, w, *, first_col_block, n_col_blocks, tm, tn, name, act=_identity):
    m, k = a.shape
    return pl.pallas_call(
        functools.partial(_mm_wcast_kernel, act=act),
        grid=(n_col_blocks, m // tm),
        in_specs=[pl.BlockSpec((tm, k), lambda j, i: (i, 0)),
                  pl.BlockSpec((k, tn), lambda j, i: (0, first_col_block + j))],
        out_specs=pl.BlockSpec((tm, tn), lambda j, i: (i, j)),
        out_shape=jax.ShapeDtypeStruct((m, n_col_blocks * tn), F32),
        scratch_shapes=[pltpu.VMEM((k, tn), BF16)],
        compiler_params=_params("parallel", "arbitrary"),
        name=name,
    )(a, w)


def _ssd_gate_norm(y, xs, zs, dexp, normw):
    yg = (y + dexp * xs) * zs
    gw = D_INNER // GROUPS
    outs = []
    for g in range(GROUPS):
        blk = yg[:, g * gw:(g + 1) * gw]
        ms = jnp.mean(blk * blk, axis=-1, keepdims=True)
        outs.append(blk * lax.rsqrt(ms + RMS_EPS) * normw[:, g * gw:(g + 1) * gw])
    return outs


def _dot_exact01(x, r01):
    hi = x.astype(BF16)
    rest = x - hi.astype(F32)
    mid = rest.astype(BF16)
    lo = (rest - mid.astype(F32)).astype(BF16)
    return (_dot(hi, r01) + _dot(mid, r01)) + _dot(lo, r01)


def _ssd_kernel(xbc_ref, dtr_ref, zs_ref, convw_ref, convb_ref, dtb_ref, alog_ref,
                dexp_ref, normw_ref, ys_ref, xss_ref,
                y_ref, convst_ref, ssm_ref,
                xext, xc_s, rexp_s, rcol_s, colb_s, acumt_s, dtt_s, wt_s, stt_s, ysc_s):
    i = pl.program_id(0)
    gw = D_INNER // GROUPS

    @pl.when(i == 0)
    def _():
        hrow = lax.broadcasted_iota(jnp.int32, (DT_PAD, D_INNER), 0)
        col = lax.broadcasted_iota(jnp.int32, (DT_PAD, D_INNER), 1)
        rexp_s[...] = jnp.where(lax.shift_right_logical(col, 6) == hrow, 1.0, 0.0).astype(BF16)
        hrow = lax.broadcasted_iota(jnp.int32, (DT_PAD, HEADS * CHUNK), 0)
        col = lax.broadcasted_iota(jnp.int32, (DT_PAD, HEADS * CHUNK), 1)
        rcol_s[...] = jnp.where(lax.shift_right_logical(col, 7) == hrow, 1.0, 0.0).astype(BF16)

    @pl.when(i == BATCH * N_CHUNKS)
    def _():
        outs = _ssd_gate_norm(ys_ref[...], xss_ref[...], zs_ref[...], dexp_ref[...], normw_ref[...])
        for g in range(GROUPS):
            y_ref[:, g * gw:(g + 1) * gw] = outs[g].astype(y_ref.dtype)

    @pl.when(i < BATCH * N_CHUNKS)
    def _():
        _ssd_prompt_chunk(i % N_CHUNKS, xbc_ref, dtr_ref, zs_ref, convw_ref, convb_ref, dtb_ref, alog_ref,
                          dexp_ref, normw_ref, y_ref, convst_ref, ssm_ref,
                          xext, xc_s, rexp_s, rcol_s, colb_s, acumt_s, dtt_s, wt_s, stt_s, ysc_s)


def _ssd_prompt_chunk(c, xbc_ref, dtr_ref, zs_ref, convw_ref, convb_ref, dtb_ref, alog_ref,
                      dexp_ref, normw_ref, y_ref, convst_ref, ssm_ref,
                      xext, xc_s, rexp_s, rcol_s, colb_s, acumt_s, dtt_s, wt_s, stt_s, ysc_s):
    gw = D_INNER // GROUPS

    @pl.when(c == 0)
    def _():
        xext[0:8, :] = jnp.zeros((8, CONV_DIM), F32)
        stt_s[...] = jnp.zeros(stt_s.shape, F32)

    xext[8:8 + CHUNK, :] = xbc_ref[...]
    slab = 512
    for s0 in range(0, CONV_DIM, slab):
        sl = slice(s0, s0 + slab)
        ext = xext[:, sl]
        acc = convw_ref[0:1, sl] * ext
        acc = convw_ref[1:2, sl] * ext + pltpu.roll(acc, 1, axis=0)
        acc = convw_ref[2:3, sl] * ext + pltpu.roll(acc, 1, axis=0)
        acc = convw_ref[3:4, sl] * ext + pltpu.roll(acc, 1, axis=0)
        xc_s[:, sl] = _silu(convb_ref[:, sl] + acc[8:8 + CHUNK, :])
    convst_ref[0] = xbc_ref[CHUNK - 3:CHUNK, :]
    xext[0:8, :] = xbc_ref[CHUNK - 8:CHUNK, :]

    dtv = _softplus(dtr_ref[...] + dtb_ref[...])
    a = dtv * (-jnp.exp(alog_ref[...]))
    row = lax.broadcasted_iota(jnp.int32, (CHUNK, CHUNK), 0)
    colq = lax.broadcasted_iota(jnp.int32, (CHUNK, CHUNK), 1)
    causal = row >= colq
    acum = _dot_f32(jnp.where(causal, 1.0, 0.0).astype(F32), a)
    last = acum[CHUNK - 1:CHUNK, :]
    acumt_s[...] = acum.T
    dtt_s[...] = dtv.T
    wt_s[...] = (dtv * jnp.exp(last - acum)).T
    colb_s[...] = _dot_exact01(acum, rcol_s[...])
    chunk_decay = jnp.exp(_dot_exact01(jnp.broadcast_to(last, (8, DT_PAD)), rexp_s[...]))[0:1, :]

    lane = lax.broadcasted_iota(jnp.int32, (CHUNK, 128), 1)
    lo = lane < HEADDIM
    zero_b = jnp.zeros((CHUNK, 128), BF16)
    for g in range(GROUPS):
        bg = xc_s[:, B_OFF + g * STATE:B_OFF + (g + 1) * STATE]
        cg = xc_s[:, C_OFF + g * STATE:C_OFF + (g + 1) * STATE]
        cb = lax.dot_general(cg.astype(BF16), bg.astype(BF16), (((1,), (1,)), ((), ())),
                             preferred_element_type=F32)
        bgt = bg.T
        for pr in range(HEADS_PER_GROUP // 2):
            hp = g * (HEADS_PER_GROUP // 2) + pr
            psl = slice(hp * 128, (hp + 1) * 128)
            heads = (2 * hp, 2 * hp + 1)
            lhs = []
            for hh in heads:
                colb = colb_s[:, hh * CHUNK:(hh + 1) * CHUNK]
                seg = colb - acumt_s[hh:hh + 1, :]
                decay = jnp.exp(jnp.where(causal, seg, NEG_INF))
                lhs.append((cb * decay * dtt_s[hh:hh + 1, :]).astype(BF16))
            for hh in heads:
                colb = colb_s[:, hh * CHUNK:(hh + 1) * CHUNK]
                lhs.append((cg * jnp.exp(colb)).astype(BF16))
            xpair = xc_s[:, psl].astype(BF16)
            spair = stt_s[hp].astype(BF16)
            x_lo, x_hi = jnp.where(lo, xpair, zero_b), jnp.where(lo, zero_b, xpair)
            rhs = jnp.concatenate([x_lo, x_hi, jnp.where(lo, spair, zero_b), jnp.where(lo, zero_b, spair)],
                                  axis=0)
            ysc_s[:, psl] = _dot(jnp.concatenate(lhs, axis=1), rhs)
            bw = jnp.concatenate([(bgt * wt_s[hh:hh + 1, :]).astype(BF16) for hh in heads], axis=1)
            upd = _dot(bw, jnp.concatenate([x_lo, x_hi], axis=0))
            stt_s[hp] = stt_s[hp] * chunk_decay[:, psl] + upd

    outs = _ssd_gate_norm(ysc_s[...], xc_s[:, 0:D_INNER], zs_ref[...], dexp_ref[...], normw_ref[...])
    for g in range(GROUPS):
        y_ref[:, g * gw:(g + 1) * gw] = outs[g].astype(y_ref.dtype)

    @pl.when(c == N_CHUNKS - 1)
    def _():
        for hp in range(HEAD_PAIRS):
            st = stt_s[hp].T
            ssm_ref[0, 2 * hp] = st[0:HEADDIM, :]
            ssm_ref[0, 2 * hp + 1] = st[HEADDIM:2 * HEADDIM, :]


def _ssd(xbc, dtr, zs, convw, convb, dtb, alog, dexp, normw, y_sample, xs_sample):
    rowblk = lambda i: (i, 0)
    const = lambda i: (0, 0)
    seq = lambda i: jnp.minimum(i // N_CHUNKS, BATCH - 1)
    return pl.pallas_call(
        _ssd_kernel,
        grid=(BATCH * N_CHUNKS + 1,),
        in_specs=[pl.BlockSpec((CHUNK, CONV_DIM), rowblk),
                  pl.BlockSpec((CHUNK, DT_PAD), rowblk),
                  pl.BlockSpec((CHUNK, D_INNER), rowblk),
                  pl.BlockSpec((CONV_K, CONV_DIM), const),
                  pl.BlockSpec((1, CONV_DIM), const),
                  pl.BlockSpec((1, DT_PAD), const),
                  pl.BlockSpec((1, DT_PAD), const),
                  pl.BlockSpec((1, D_INNER), const),
                  pl.BlockSpec((1, D_INNER), const),
                  pl.BlockSpec((DEC_BATCH, D_INNER), const),
                  pl.BlockSpec((DEC_BATCH, D_INNER), const)],
        out_specs=[pl.BlockSpec((CHUNK, D_INNER), rowblk),
                   pl.BlockSpec((1, CONV_K - 1, CONV_DIM), lambda i: (seq(i), 0, 0)),
                   pl.BlockSpec((1, HEADS, HEADDIM, STATE), lambda i: (seq(i), 0, 0, 0))],
        out_shape=[jax.ShapeDtypeStruct((T_ALL, D_INNER), BF16),
                   jax.ShapeDtypeStruct((BATCH, CONV_K - 1, CONV_DIM), F32),
                   jax.ShapeDtypeStruct((BATCH, HEADS, HEADDIM, STATE), F32)],
        scratch_shapes=[pltpu.VMEM((8 + CHUNK, CONV_DIM), F32),
                        pltpu.VMEM((CHUNK, CONV_DIM), F32),
                        pltpu.VMEM((DT_PAD, D_INNER), BF16),
                        pltpu.VMEM((DT_PAD, HEADS * CHUNK), BF16),
                        pltpu.VMEM((CHUNK, HEADS * CHUNK), F32),
                        pltpu.VMEM((DT_PAD, CHUNK), F32),
                        pltpu.VMEM((DT_PAD, CHUNK), F32),
                        pltpu.VMEM((DT_PAD, CHUNK), F32),
                        pltpu.VMEM((HEAD_PAIRS, STATE, 128), F32),
                        pltpu.VMEM((CHUNK, D_INNER), F32)],
        compiler_params=_params("arbitrary"),
        name="ssd",
    )(xbc, dtr, zs, convw, convb, dtb, alog, dexp, normw, y_sample, xs_sample)


def _sgu_kernel(uvg_ref, lng_ref, lnb_ref, w_ref, bfull_ref, w0_ref, b0_ref, yb_ref, v_ref, vs_ref):
    i = pl.program_id(0)
    vn = _layer_norm(uvg_ref[:, D_GATE:2 * D_GATE], lng_ref[...], lnb_ref[...])

    @pl.when(i == BATCH * N_CHUNKS)
    def _():
        vs_ref[...] = vn
        s = w0_ref[...] * vn + b0_ref[...]
        yb_ref[...] = (uvg_ref[:, 0:D_GATE] * s).astype(yb_ref.dtype)

    @pl.when(i < BATCH * N_CHUNKS)
    def _():
        v_ref[0] = vn
        row = lax.broadcasted_iota(jnp.int32, (CHUNK, CHUNK), 0)
        col = lax.broadcasted_iota(jnp.int32, (CHUNK, CHUNK), 1)
        causal = row >= col
        for g in range(MLP_GROUPS):
            sl = slice(g * 128, (g + 1) * 128)
            wg = jnp.where(causal, w_ref[g], 0.0).astype(BF16)
            s = _dot(wg, vn[:, sl].astype(BF16)) + bfull_ref[:, sl]
            yb_ref[:, sl] = (uvg_ref[:, sl] * s).astype(yb_ref.dtype)


def _sgu(uvg, lng, lnb, w, bfull, w0, b0):
    const2 = lambda i: (0, 0)
    seq = lambda i: jnp.minimum(i // N_CHUNKS, BATCH - 1)
    return pl.pallas_call(
        _sgu_kernel,
        grid=(BATCH * N_CHUNKS + 1,),
        in_specs=[pl.BlockSpec((CHUNK, 2 * D_GATE), lambda i: (i, 0)),
                  pl.BlockSpec((1, D_GATE), const2),
                  pl.BlockSpec((1, D_GATE), const2),
                  pl.BlockSpec((MLP_GROUPS, CHUNK, CHUNK), lambda i: (0, 0, 0)),
                  pl.BlockSpec((CHUNK, D_GATE), const2),
                  pl.BlockSpec((1, D_GATE), const2),
                  pl.BlockSpec((1, D_GATE), const2)],
        out_specs=[pl.BlockSpec((CHUNK, D_GATE), lambda i: (i, 0)),
                   pl.BlockSpec((1, CHUNK, D_GATE), lambda i: (seq(i), 0, 0)),
                   pl.BlockSpec((DEC_BATCH, D_GATE), const2)],
        out_shape=[jax.ShapeDtypeStruct((T_ALL, D_GATE), BF16),
                   jax.ShapeDtypeStruct((BATCH, CHUNK, D_GATE), F32),
                   jax.ShapeDtypeStruct((DEC_BATCH, D_GATE), F32)],
        compiler_params=_params("arbitrary"),
        name="sgu",
    )(uvg, lng, lnb, w, bfull, w0, b0)


def _sample_pre_kernel(xbc_ref, sc_ref, dtr_ref, convw_ref, convb_ref, dtb_ref, alog_ref,
                       convnew_ref, xs_ref, xdt_ref, bm_ref, cm_ref, dec_ref):
    x = xbc_ref[...]
    acc = convb_ref[...] + convw_ref[3:4, :] * x
    acc = acc + convw_ref[2:3, :] * sc_ref[2]
    acc = acc + convw_ref[1:2, :] * sc_ref[1]
    acc = acc + convw_ref[0:1, :] * sc_ref[0]
    xc = _silu(acc)
    convnew_ref[0] = sc_ref[1]
    convnew_ref[1] = sc_ref[2]
    convnew_ref[2] = x
    dtv = _softplus(dtr_ref[...] + dtb_ref[...])
    dec_ref[...] = jnp.exp(dtv * (-jnp.exp(alog_ref[...])))
    hrow = lax.broadcasted_iota(jnp.int32, (DT_PAD, D_INNER), 0)
    col = lax.broadcasted_iota(jnp.int32, (DT_PAD, D_INNER), 1)
    rexp = jnp.where(lax.shift_right_logical(col, 6) == hrow, 1.0, 0.0).astype(F32)
    xs = xc[:, 0:D_INNER]
    xs_ref[...] = xs
    xdt_ref[...] = xs * _dot_f32(dtv, rexp)
    bm_ref[...] = xc[:, B_OFF:C_OFF]
    cm_ref[...] = xc[:, C_OFF:CONV_DIM]


def _sample_pre(xbc, sc_t, dtr, convw, convb, dtb, alog):
    blk = lambda i: (SAMPLE_BLOCK, 0)
    const = lambda i: (0, 0)
    return pl.pallas_call(
        _sample_pre_kernel,
        grid=(1,),
        in_specs=[pl.BlockSpec((DEC_BATCH, CONV_DIM), blk),
                  pl.BlockSpec((CONV_K - 1, DEC_BATCH, CONV_DIM), lambda i: (0, 0, 0)),
                  pl.BlockSpec((DEC_BATCH, DT_PAD), blk),
                  pl.BlockSpec((CONV_K, CONV_DIM), const),
                  pl.BlockSpec((1, CONV_DIM), const),
                  pl.BlockSpec((1, DT_PAD), const),
                  pl.BlockSpec((1, DT_PAD), const)],
        out_specs=[pl.BlockSpec((CONV_K - 1, DEC_BATCH, CONV_DIM), lambda i: (0, 0, 0)),
                   pl.BlockSpec((DEC_BATCH, D_INNER), const),
                   pl.BlockSpec((DEC_BATCH, D_INNER), const),
                   pl.BlockSpec((DEC_BATCH, GROUPS * STATE), const),
                   pl.BlockSpec((DEC_BATCH, GROUPS * STATE), const),
                   pl.BlockSpec((DEC_BATCH, DT_PAD), const)],
        out_shape=[jax.ShapeDtypeStruct((CONV_K - 1, DEC_BATCH, CONV_DIM), F32),
                   jax.ShapeDtypeStruct((DEC_BATCH, D_INNER), F32),
                   jax.ShapeDtypeStruct((DEC_BATCH, D_INNER), F32),
                   jax.ShapeDtypeStruct((DEC_BATCH, GROUPS * STATE), F32),
                   jax.ShapeDtypeStruct((DEC_BATCH, GROUPS * STATE), F32),
                   jax.ShapeDtypeStruct((DEC_BATCH, DT_PAD), F32)],
        compiler_params=_params("arbitrary"),
        name="sample_pre",
    )(xbc, sc_t, dtr, convw, convb, dtb, alog)


def _sample_state_kernel(dec_ref, xdtt_ref, bm_ref, cmt_ref, st_ref, new_ref, yt_ref):
    srow = lax.broadcasted_iota(jnp.int32, (DEC_BATCH, STATE), 0)
    slane = lax.broadcasted_iota(jnp.int32, (STATE, DEC_BATCH), 1)
    gw = HEADS_PER_GROUP * HEADDIM

    @pl.when(pl.program_id(0) == 0)
    def _():
        yt_ref[...] = jnp.zeros(yt_ref.shape, F32)

    for i in range(SAMPLES_PER_STEP):
        s = pl.program_id(0) * SAMPLES_PER_STEP + i
        brow = bm_ref[pl.ds(s, 1), :]
        for g in range(GROUPS):
            sel_b = jnp.where(srow == s, brow[:, g * STATE:(g + 1) * STATE], 0.0).astype(BF16)
            upd = _dot(xdtt_ref[g * gw:(g + 1) * gw, :], sel_b)
            news = []
            for r in range(HEADS_PER_GROUP):
                h = g * HEADS_PER_GROUP + r
                rows = slice(h * HEADDIM, (h + 1) * HEADDIM)
                new = st_ref[i, rows, :] * dec_ref[s, h] + upd[r * HEADDIM:(r + 1) * HEADDIM, :]
                new_ref[i, rows, :] = new
                news.append(new.astype(BF16))
            sel_c = jnp.where(slane == s, cmt_ref[g * STATE:(g + 1) * STATE, :], 0.0).astype(BF16)
            yt_ref[g * gw:(g + 1) * gw, :] += _dot(jnp.concatenate(news, axis=0), sel_c)


def _sample_state(dec, xdtt, bm, cmt, state):
    const = lambda s: (0, 0)
    blk = pl.BlockSpec((SAMPLES_PER_STEP, D_INNER, STATE), lambda s: (s, 0, 0))
    return pl.pallas_call(
        _sample_state_kernel,
        grid=(DEC_BATCH // SAMPLES_PER_STEP,),
        in_specs=[pl.BlockSpec(memory_space=pltpu.SMEM),
                  pl.BlockSpec((D_INNER, DEC_BATCH), const),
                  pl.BlockSpec((DEC_BATCH, GROUPS * STATE), const),
                  pl.BlockSpec((GROUPS * STATE, DEC_BATCH), const),
                  blk],
        out_specs=[blk, pl.BlockSpec((D_INNER, DEC_BATCH), const)],
        out_shape=[jax.ShapeDtypeStruct((DEC_BATCH, D_INNER, STATE), F32),
                   jax.ShapeDtypeStruct((D_INNER, DEC_BATCH), F32)],
        compiler_params=_params("arbitrary"),
        name="sample_state",
    )(dec, xdtt, bm, cmt, state)


def _merge_kernel(ya_ref, wa_ref, yb_ref, wb_ref, ga_ref, gb_ref, o_ref):
    o_ref[...] = (ga_ref[...] * _dot(ya_ref[...], wa_ref[...])
                  + gb_ref[...] * _dot(yb_ref[...], wb_ref[...])).astype(o_ref.dtype)


def _merge(ya, wa, yb, wb, gates):
    tm, tn = MM_TM, MM_TN // 2
    nb = D_MODEL // tn
    return pl.pallas_call(
        _merge_kernel,
        grid=(T_ALL // tm, nb),
        in_specs=[pl.BlockSpec((tm, D_INNER), lambda i, j: (i, 0)),
                  pl.BlockSpec((D_INNER, tn), lambda i, j: (0, j)),
                  pl.BlockSpec((tm, D_GATE), lambda i, j: (i, 0)),
                  pl.BlockSpec((D_GATE, tn), lambda i, j: (0, j)),
                  pl.BlockSpec((tm, tn), lambda i, j: (i, j)),
                  pl.BlockSpec((tm, tn), lambda i, j: (i, j + nb))],
        out_specs=pl.BlockSpec((tm, tn), lambda i, j: (i, j)),
        out_shape=jax.ShapeDtypeStruct((T_ALL, D_MODEL), BF16),
        compiler_params=_params("parallel", "parallel"),
        name="merge",
    )(ya, wa, yb, wb, gates, gates)


def _out_ln_kernel(m_ref, w_ref, x_ref, g_ref, b_ref, h_ref, hb_ref):
    h = _layer_norm(DN_ALPHA * x_ref[...] + _dot(m_ref[...], w_ref[...]), g_ref[...], b_ref[...])
    h_ref[...] = h
    hb_ref[...] = h.astype(BF16)


def _out_ln(merged, wout, x, g, b):
    tm = TOKEN_TILE
    const = lambda i: (0, 0)
    rows = lambda i: (i, 0)
    return pl.pallas_call(
        _out_ln_kernel,
        grid=(T_ALL // tm,),
        in_specs=[pl.BlockSpec((tm, D_MODEL), rows),
                  pl.BlockSpec((D_MODEL, D_MODEL), const),
                  pl.BlockSpec((tm, D_MODEL), rows),
                  pl.BlockSpec((1, D_MODEL), const),
                  pl.BlockSpec((1, D_MODEL), const)],
        out_specs=[pl.BlockSpec((tm, D_MODEL), rows), pl.BlockSpec((tm, D_MODEL), rows)],
        out_shape=[jax.ShapeDtypeStruct((T_ALL, D_MODEL), F32),
                   jax.ShapeDtypeStruct((T_ALL, D_MODEL), BF16)],
        compiler_params=_params("parallel"),
        name="out_ln",
    )(merged, wout, x, g, b)


def _final_ln_kernel(h_ref, p_ref, g_ref, b_ref, o_ref):
    o_ref[...] = _layer_norm(DN_ALPHA * h_ref[...] + p_ref[...], g_ref[...], b_ref[...])


def _final_ln(h, p, g, b, *, tm, first_block, n_blocks, name):
    const = lambda i: (0, 0)
    rows = lambda i: (first_block + i, 0)
    return pl.pallas_call(
        _final_ln_kernel,
        grid=(n_blocks,),
        in_specs=[pl.BlockSpec((tm, D_MODEL), rows), pl.BlockSpec((tm, D_MODEL), rows),
                  pl.BlockSpec((1, D_MODEL), const), pl.BlockSpec((1, D_MODEL), const)],
        out_specs=pl.BlockSpec((tm, D_MODEL), lambda i: (i, 0)),
        out_shape=jax.ShapeDtypeStruct((n_blocks * tm, D_MODEL), F32),
        compiler_params=_params("parallel"),
        name=name,
    )(h, p, g, b)


def _peer_front_kernel(qt_ref, keys_ref, s1_ref, e1_ref, s2_ref, e2_ref, thr_ref):
    k = PEER_TOPK
    for h in range(PEER_HEADS):
        sc = []
        top = []
        for side in range(2):
            hs = 2 * h + side
            x = _dot(keys_ref[hs], qt_ref[hs * PEER_KEYS:(hs + 1) * PEER_KEYS, :])
            sc.append(x)
            vals = []
            for _ in range(k):
                m = jnp.max(x, axis=0, keepdims=True)
                vals.append(m)
                x = jnp.where(x == m, NEG_INF, x)
            top.append(vals)
        cands = [top[0][i] + top[1][j] for i in range(k) for j in range(k) if (i + 1) * (j + 1) <= k]
        pad = (-len(cands)) % 8
        tt = cands[0].shape[1]
        cand = jnp.concatenate(cands + [jnp.full((pad, tt), NEG_INF, F32)], axis=0)
        idx = lax.broadcasted_iota(jnp.int32, cand.shape, 0)
        best = []
        for _ in range(k):
            m = jnp.max(cand, axis=0, keepdims=True)
            best.append(m)
            first = jnp.min(jnp.where(cand == m, idx, cand.shape[0]), axis=0, keepdims=True)
            cand = jnp.where(idx == first, NEG_INF, cand)
        z = jnp.zeros_like(best[0])
        for bk in best:
            z = z + jnp.exp(bk - best[0])
        s1_ref[h] = sc[0]
        s2_ref[h] = sc[1]
        e1_ref[h] = jnp.exp(sc[0] - top[0][0]) * (1.0 / z)
        e2_ref[h] = jnp.exp(sc[1] - top[1][0])
        thr_ref[h:h + 1, :] = best[k - 1]


def _peer_front(qt, keys):
    tt = 128
    blk3 = pl.BlockSpec((PEER_HEADS, PEER_KEYS, tt), lambda i: (0, 0, i))
    shp3 = jax.ShapeDtypeStruct((PEER_HEADS, PEER_KEYS, T_PAD), F32)
    return pl.pallas_call(
        _peer_front_kernel,
        grid=(T_PAD // tt,),
        in_specs=[pl.BlockSpec((2 * PEER_HEADS * PEER_KEYS, tt), lambda i: (0, i)),
                  pl.BlockSpec((2 * PEER_HEADS, PEER_KEYS, PEER_KEYS), lambda i: (0, 0, 0))],
        out_specs=[blk3, blk3, blk3, blk3, pl.BlockSpec((PEER_HEADS, tt), lambda i: (0, i))],
        out_shape=[shp3, shp3, shp3, shp3, jax.ShapeDtypeStruct((PEER_HEADS, T_PAD), F32)],
        compiler_params=_params("parallel"),
        name="peer_front",
    )(qt, keys)


def _peer_main_kernel(ht_ref, u_ref, v_ref, s1_ref, e1_ref, s2_ref, e2_ref, thr_ref, o_ref):
    j = pl.program_id(1)
    tt = ht_ref.shape[1]
    n_i1 = PEER_SUB // PEER_KEYS
    hk = PEER_GATE_ROWS
    n_chunks = tt // PEER_CHUNK
    chains = [(sub, c) for sub in range(PEER_EB // PEER_SUB) for c in range(n_chunks)]
    i1_rows = pl.ds(pl.multiple_of(j * (PEER_EB // PEER_KEYS), PEER_EB // PEER_KEYS), PEER_EB // PEER_KEYS)

    @pl.when(j == 0)
    def _():
        o_ref[...] = jnp.zeros(o_ref.shape, F32)

    def activations(k):
        sub, c = chains[k]
        return _dot(u_ref[sub * PEER_SUB:(sub + 1) * PEER_SUB, :],
                    ht_ref[:, c * PEER_CHUNK:(c + 1) * PEER_CHUNK])

    def gating(k, act):
        sub, c = chains[k]
        tok_tiles = []
        for tc in range(PEER_CHUNK // 128):
            cols = slice(c * PEER_CHUNK + tc * 128, c * PEER_CHUNK + (tc + 1) * 128)
            lcols = slice(tc * 128, (tc + 1) * 128)
            parts = []
            for part in range(PEER_KEYS // hk):
                krows = slice(part * hk, (part + 1) * hk)
                gates = [jnp.zeros((hk, 128), F32) for _ in range(n_i1)]
                for h in range(PEER_HEADS):
                    s2t = s2_ref[h, krows, cols]
                    e2t = e2_ref[h, krows, cols]
                    thr = thr_ref[h:h + 1, cols]
                    s1t = s1_ref[h, i1_rows, cols]
                    e1t = e1_ref[h, i1_rows, cols]
                    for r in range(n_i1):
                        q = sub * n_i1 + r
                        pair = s1t[q:q + 1, :] + s2t
                        w = e1t[q:q + 1, :] * e2t
                        gates[r] = gates[r] + jnp.where(pair >= thr, w, 0.0)
                for r in range(n_i1):
                    arows = slice(r * PEER_KEYS + part * hk, r * PEER_KEYS + (part + 1) * hk)
                    gates[r] = gates[r] * _gelu(act[arows, lcols])
                parts.append(gates)
            tiles = [jnp.concatenate([p[r] for p in parts], axis=0).T.astype(BF16)
                     for r in range(n_i1)]
            tok_tiles.append(jnp.concatenate(tiles, axis=1))
        return jnp.concatenate(tok_tiles, axis=0)

    def accumulate(k, gt):
        sub, c = chains[k]
        o_ref[c * PEER_CHUNK:(c + 1) * PEER_CHUNK, :] += _dot(
            gt, v_ref[sub * PEER_SUB:(sub + 1) * PEER_SUB, :])

    act = activations(0)
    for k in range(len(chains)):
        act_next = activations(k + 1) if k + 1 < len(chains) else None
        accumulate(k, gating(k, act))
        act = act_next


def _peer_main(ht, u, v, s1, e1, s2, e2, thr):
    tt, eb = PEER_TT, PEER_EB
    once = pl.Buffered(1)
    blk3 = pl.BlockSpec((PEER_HEADS, PEER_KEYS, tt), lambda i, j: (0, 0, i), pipeline_mode=once)
    return pl.pallas_call(
        _peer_main_kernel,
        grid=(T_PAD // tt, PEER_EXPERTS // eb),
        in_specs=[pl.BlockSpec((D_MODEL, tt), lambda i, j: (0, i), pipeline_mode=once),
                  pl.BlockSpec((eb, D_MODEL), lambda i, j: (j, 0)),
                  pl.BlockSpec((eb, D_MODEL), lambda i, j: (j, 0)),
                  blk3, blk3, blk3, blk3,
                  pl.BlockSpec((PEER_HEADS, tt), lambda i, j: (0, i), pipeline_mode=once)],
        out_specs=pl.BlockSpec((tt, D_MODEL), lambda i, j: (i, 0)),
        out_shape=jax.ShapeDtypeStruct((T_PAD, D_MODEL), F32),
        compiler_params=_params("parallel", "arbitrary"),
        name="peer_main",
    )(ht, u, v, s1, e1, s2, e2, thr)


def kernel(x_prompt, x_sample, state_conv, state_ssm, w_in, conv_w, conv_b, dt_bias, a_log, d_skip,
           ssd_norm_w, sgu_ln_g, sgu_ln_b, sgu_w, sgu_b, w_branch_a, w_branch_b, w_out, ln1_g, ln1_b,
           peer_wq, peer_keys, peer_u, peer_v, ln2_g, ln2_b):
    row = lambda p: p[0].reshape(1, -1)
    x = jnp.concatenate([x_prompt.reshape(T_PROMPT, D_MODEL), x_sample.reshape(DEC_BATCH, D_MODEL)], axis=0)
    xb = x.astype(BF16)

    w = w_in[0]
    o1, o2, o3, o4 = D_INNER, D_INNER + CONV_DIM, D_INNER + CONV_DIM + HEADS, D_INNER + CONV_DIM + HEADS + 2 * D_GATE
    w_dt = jnp.pad(w[:, o2:o3], ((0, 0), (0, DT_PAD - HEADS)))
    mm = functools.partial(_matmul, tm=MM_TM)
    zs = _matmul_wcast(xb, w, first_col_block=0, n_col_blocks=D_INNER // MM_TN, tm=MM_TM, tn=MM_TN,
                       act=_silu, name="proj_z")
    xbc = _matmul_wcast(xb, w, first_col_block=o1 // MM_TN, n_col_blocks=CONV_DIM // MM_TN, tm=MM_TM,
                        tn=MM_TN, name="proj_xbc")
    dtr = mm(xb, w_dt.astype(BF16), tn=DT_PAD, name="proj_dt")
    uvg = mm(xb, w[:, o3:o4].astype(BF16), tn=MM_TN, act=_gelu, name="proj_uv")
    gates = mm(xb, w[:, o4:].astype(BF16), tn=MM_TN, act=jax.nn.sigmoid, name="proj_gates")

    pad_h = lambda p: jnp.pad(p[0], (0, DT_PAD - HEADS)).reshape(1, DT_PAD)
    dtb, alog = pad_h(dt_bias), pad_h(a_log)
    dexp = jnp.repeat(d_skip[0], HEADDIM).reshape(1, D_INNER)
    convw, convb, normw = conv_w[0], row(conv_b), row(ssd_norm_w)
    lng, lnb = row(sgu_ln_g), row(sgu_ln_b)

    sc_t = jnp.transpose(state_conv[0], (1, 0, 2))
    conv_s_t, xs_s, xdt_s, bm_s, cm_s, dec_s = _sample_pre(xbc, sc_t, dtr, convw, convb, dtb, alog)
    ssm_s, yt_s = _sample_state(dec_s, xdt_s.T.astype(BF16), bm_s, cm_s.T,
                                state_ssm[0].reshape(DEC_BATCH, D_INNER, STATE))

    ya, conv_p, ssm_p = _ssd(xbc, dtr, zs, convw, convb, dtb, alog, dexp, normw, yt_s.T, xs_s)
    bfull = jnp.repeat(sgu_b[0].T, CHUNK, axis=1)
    w0 = jnp.repeat(sgu_w[0][:, 0, 0], CHUNK).reshape(1, D_GATE)
    b0 = jnp.repeat(sgu_b[0][:, 0], CHUNK).reshape(1, D_GATE)
    yb, v_p, v_s = _sgu(uvg, lng, lnb, sgu_w[0], bfull, w0, b0)

    merged = _merge(ya, w_branch_a[0].astype(BF16), yb, w_branch_b[0].astype(BF16), gates)
    h, hb = _out_ln(merged, w_out[0].astype(BF16), x, row(ln1_g), row(ln1_b))

    ht = jnp.pad(hb.T, ((0, 0), (0, T_PAD - T_ALL)))
    qt = _matmul(peer_wq[0].T.astype(BF16), ht, tm=1024, tn=PEER_TT, out_dtype=BF16, name="peer_query")
    keys = peer_keys[0].reshape(2 * PEER_HEADS, PEER_KEYS, PEER_KEYS).astype(BF16)
    s1, e1, s2, e2, thr = _peer_front(qt, keys)
    p = _peer_main(ht, peer_u[0].astype(BF16), peer_v[0].astype(BF16), s1, e1, s2, e2, thr)
    y_prompt = _final_ln(h, p, row(ln2_g), row(ln2_b), tm=512, first_block=0, n_blocks=T_PROMPT // 512,
                         name="final_ln_prompt").reshape(BATCH, SEQ, D_MODEL)
    y_sample = _final_ln(h, p, row(ln2_g), row(ln2_b), tm=DEC_BATCH, first_block=SAMPLE_BLOCK, n_blocks=1,
                         name="final_ln_sample").reshape(DEC_BATCH, 1, D_MODEL)
    conv_sample = jnp.transpose(conv_s_t, (1, 0, 2))[None]
    return (y_prompt, y_sample, conv_p[None], ssm_p[None], v_p[None],
            conv_sample, ssm_s.reshape(1, DEC_BATCH, HEADS, HEADDIM, STATE), v_s[None, :, None, :])
```

```python
import functools
import math

import jax
import jax.numpy as jnp
from jax import lax
from jax.experimental import pallas as pl
from jax.experimental.pallas import tpu as pltpu

F32 = jnp.float32
BF16 = jnp.bfloat16

D_MODEL = 2048
SEQ = 2048
BATCH = 4
DEC_BATCH = 128
T_PROMPT = BATCH * SEQ
T_ALL = T_PROMPT + DEC_BATCH
CHUNK = 128
N_CHUNKS = SEQ // CHUNK
SAMPLE_BLOCK = T_PROMPT // CHUNK

D_INNER = 4096
HEADDIM = 64
HEADS = 64
GROUPS = 8
HEADS_PER_GROUP = HEADS // GROUPS
STATE = 128
CONV_K = 4
CONV_DIM = D_INNER + 2 * GROUPS * STATE
B_OFF = D_INNER
C_OFF = D_INNER + GROUPS * STATE
HEAD_PAIRS = HEADS // 2
DT_PAD = 128
D_GATE = 2048
MLP_GROUPS = 16
PEER_HEADS = 8
PEER_KEYS = 128
PEER_EXPERTS = PEER_KEYS * PEER_KEYS
PEER_TOPK = 16
DN_ALPHA = 2.0 ** 0.25
LN_EPS = 1e-5
RMS_EPS = 1e-5
NEG_INF = float("-inf")

TOKEN_TILE = 640
MM_TM = 1040
MM_TN = 1024
SAMPLES_PER_STEP = 2
PEER_EB = 1024
PEER_SUB = 512
PEER_CHUNK = 256
PEER_GATE_ROWS = 64
PEER_TT = 768
T_PAD = 8448
VMEM_LIMIT = 56 * 1024 * 1024


def _params(*sem):
    return pltpu.CompilerParams(dimension_semantics=sem, vmem_limit_bytes=VMEM_LIMIT)


def _gelu(x):
    return 0.5 * x * (1.0 + lax.erf(x * (1.0 / math.sqrt(2.0))))


def _silu(x):
    return x * jax.nn.sigmoid(x)


def _identity(x):
    return x


def _softplus(x):
    return jnp.maximum(x, 0.0) + jnp.log1p(jnp.exp(-jnp.abs(x)))


def _layer_norm(x, g, b):
    mu = jnp.mean(x, axis=-1, keepdims=True)
    xc = x - mu
    var = jnp.mean(xc * xc, axis=-1, keepdims=True)
    return xc * lax.rsqrt(var + LN_EPS) * g + b


def _dot(a, b):
    return jnp.dot(a, b, preferred_element_type=F32)


def _dot_f32(a, b):
    return jnp.dot(a, b, preferred_element_type=F32, precision=lax.Precision.HIGHEST)


def _mm_kernel(a_ref, b_ref, o_ref, *, act):
    o_ref[...] = act(_dot(a_ref[...], b_ref[...])).astype(o_ref.dtype)


def _matmul(a, b, *, tm, tn, name, act=_identity, out_dtype=F32):
    m, k = a.shape
    n = b.shape[1]
    return pl.pallas_call(
        functools.partial(_mm_kernel, act=act),
        grid=(m // tm, n // tn),
        in_specs=[pl.BlockSpec((tm, k), lambda i, j: (i, 0)),
                  pl.BlockSpec((k, tn), lambda i, j: (0, j))],
        out_specs=pl.BlockSpec((tm, tn), lambda i, j: (i, j)),
        out_shape=jax.ShapeDtypeStruct((m, n), out_dtype),
        compiler_params=_params("parallel", "parallel"),
        name=name,
    )(a, b)


def _ssd_gate_norm(y, xs, zs, dexp, normw):
    yg = (y + dexp * xs) * zs
    gw = D_INNER // GROUPS
    outs = []
    for g in range(GROUPS):
        blk = yg[:, g * gw:(g + 1) * gw]
        ms = jnp.mean(blk * blk, axis=-1, keepdims=True)
        outs.append(blk * lax.rsqrt(ms + RMS_EPS) * normw[:, g * gw:(g + 1) * gw])
    return outs


def _dot_exact01(x, r01):
    hi = x.astype(BF16)
    rest = x - hi.astype(F32)
    mid = rest.astype(BF16)
    lo = (rest - mid.astype(F32)).astype(BF16)
    return (_dot(hi, r01) + _dot(mid, r01)) + _dot(lo, r01)


def _ssd_kernel(xbc_ref, dtr_ref, zs_ref, convw_ref, convb_ref, dtb_ref, alog_ref,
                dexp_ref, normw_ref, ys_ref, xss_ref,
                y_ref, convst_ref, ssm_ref,
                xext, xc_s, rexp_s, rcol_s, colb_s, acumt_s, dtt_s, wt_s, stt_s, ysc_s):
    i = pl.program_id(0)
    gw = D_INNER // GROUPS

    @pl.when(i == 0)
    def _():
        hrow = lax.broadcasted_iota(jnp.int32, (DT_PAD, D_INNER), 0)
        col = lax.broadcasted_iota(jnp.int32, (DT_PAD, D_INNER), 1)
        rexp_s[...] = jnp.where(lax.shift_right_logical(col, 6) == hrow, 1.0, 0.0).astype(BF16)
        hrow = lax.broadcasted_iota(jnp.int32, (DT_PAD, HEADS * CHUNK), 0)
        col = lax.broadcasted_iota(jnp.int32, (DT_PAD, HEADS * CHUNK), 1)
        rcol_s[...] = jnp.where(lax.shift_right_logical(col, 7) == hrow, 1.0, 0.0).astype(BF16)

    @pl.when(i == BATCH * N_CHUNKS)
    def _():
        outs = _ssd_gate_norm(ys_ref[...], xss_ref[...], zs_ref[...], dexp_ref[...], normw_ref[...])
        for g in range(GROUPS):
            y_ref[:, g * gw:(g + 1) * gw] = outs[g].astype(y_ref.dtype)

    @pl.when(i < BATCH * N_CHUNKS)
    def _():
        _ssd_prompt_chunk(i % N_CHUNKS, xbc_ref, dtr_ref, zs_ref, convw_ref, convb_ref, dtb_ref, alog_ref,
                          dexp_ref, normw_ref, y_ref, convst_ref, ssm_ref,
                          xext, xc_s, rexp_s, rcol_s, colb_s, acumt_s, dtt_s, wt_s, stt_s, ysc_s)


def _ssd_prompt_chunk(c, xbc_ref, dtr_ref, zs_ref, convw_ref, convb_ref, dtb_ref, alog_ref,
                      dexp_ref, normw_ref, y_ref, convst_ref, ssm_ref,
                      xext, xc_s, rexp_s, rcol_s, colb_s, acumt_s, dtt_s, wt_s, stt_s, ysc_s):
    gw = D_INNER // GROUPS

    @pl.when(c == 0)
    def _():
        xext[0:8, :] = jnp.zeros((8, CONV_DIM), F32)
        stt_s[...] = jnp.zeros(stt_s.shape, F32)

    xext[8:8 + CHUNK, :] = xbc_ref[...]
    slab = 512
    for s0 in range(0, CONV_DIM, slab):
        sl = slice(s0, s0 + slab)
        ext = xext[:, sl]
        acc = convw_ref[0:1, sl] * ext
        acc = convw_ref[1:2, sl] * ext + pltpu.roll(acc, 1, axis=0)
        acc = convw_ref[2:3, sl] * ext + pltpu.roll(acc, 1, axis=0)
        acc = convw_ref[3:4, sl] * ext + pltpu.roll(acc, 1, axis=0)
        xc_s[:, sl] = _silu(convb_ref[:, sl] + acc[8:8 + CHUNK, :])
    convst_ref[0] = xbc_ref[CHUNK - 3:CHUNK, :]
    xext[0:8, :] = xbc_ref[CHUNK - 8:CHUNK, :]

    dtv = _softplus(dtr_ref[...] + dtb_ref[...])
    a = dtv * (-jnp.exp(alog_ref[...]))
    row = lax.broadcasted_iota(jnp.int32, (CHUNK, CHUNK), 0)
    colq = lax.broadcasted_iota(jnp.int32, (CHUNK, CHUNK), 1)
    causal = row >= colq
    acum = _dot_f32(jnp.where(causal, 1.0, 0.0).astype(F32), a)
    last = acum[CHUNK - 1:CHUNK, :]
    acumt_s[...] = acum.T
    dtt_s[...] = dtv.T
    wt_s[...] = (dtv * jnp.exp(last - acum)).T
    colb_s[...] = _dot_exact01(acum, rcol_s[...])
    chunk_decay = jnp.exp(_dot_exact01(jnp.broadcast_to(last, (8, DT_PAD)), rexp_s[...]))[0:1, :]

    lane = lax.broadcasted_iota(jnp.int32, (CHUNK, 128), 1)
    lo = lane < HEADDIM
    zero_b = jnp.zeros((CHUNK, 128), BF16)
    for g in range(GROUPS):
        bg = xc_s[:, B_OFF + g * STATE:B_OFF + (g + 1) * STATE]
        cg = xc_s[:, C_OFF + g * STATE:C_OFF + (g + 1) * STATE]
        cb = lax.dot_general(cg.astype(BF16), bg.astype(BF16), (((1,), (1,)), ((), ())),
                             preferred_element_type=F32)
        bgt = bg.T
        for pr in range(HEADS_PER_GROUP // 2):
            hp = g * (HEADS_PER_GROUP // 2) + pr
            psl = slice(hp * 128, (hp + 1) * 128)
            heads = (2 * hp, 2 * hp + 1)
            lhs = []
            for hh in heads:
                colb = colb_s[:, hh * CHUNK:(hh + 1) * CHUNK]
                seg = colb - acumt_s[hh:hh + 1, :]
                decay = jnp.exp(jnp.where(causal, seg, NEG_INF))
                lhs.append((cb * decay * dtt_s[hh:hh + 1, :]).astype(BF16))
            for hh in heads:
                colb = colb_s[:, hh * CHUNK:(hh + 1) * CHUNK]
                lhs.append((cg * jnp.exp(colb)).astype(BF16))
            xpair = xc_s[:, psl].astype(BF16)
            spair = stt_s[hp].astype(BF16)
            x_lo, x_hi = jnp.where(lo, xpair, zero_b), jnp.where(lo, zero_b, xpair)
            rhs = jnp.concatenate([x_lo, x_hi, jnp.where(lo, spair, zero_b), jnp.where(lo, zero_b, spair)],
                                  axis=0)
            ysc_s[:, psl] = _dot(jnp.concatenate(lhs, axis=1), rhs)
            bw = jnp.concatenate([(bgt * wt_s[hh:hh + 1, :]).astype(BF16) for hh in heads], axis=1)
            upd = _dot(bw, jnp.concatenate([x_lo, x_hi], axis=0))
            stt_s[hp] = stt_s[hp] * chunk_decay[:, psl] + upd

    outs = _ssd_gate_norm(ysc_s[...], xc_s[:, 0:D_INNER], zs_ref[...], dexp_ref[...], normw_ref[...])
    for g in range(GROUPS):
        y_ref[:, g * gw:(g + 1) * gw] = outs[g].astype(y_ref.dtype)

    @pl.when(c == N_CHUNKS - 1)
    def _():
        for hp in range(HEAD_PAIRS):
            st = stt_s[hp].T
            ssm_ref[0, 2 * hp] = st[0:HEADDIM, :]
            ssm_ref[0, 2 * hp + 1] = st[HEADDIM:2 * HEADDIM, :]


def _ssd(xbc, dtr, zs, convw, convb, dtb, alog, dexp, normw, y_sample, xs_sample):
    rowblk = lambda i: (i, 0)
    const = lambda i: (0, 0)
    seq = lambda i: jnp.minimum(i // N_CHUNKS, BATCH - 1)
    return pl.pallas_call(
        _ssd_kernel,
        grid=(BATCH * N_CHUNKS + 1,),
        in_specs=[pl.BlockSpec((CHUNK, CONV_DIM), rowblk),
                  pl.BlockSpec((CHUNK, DT_PAD), rowblk),
                  pl.BlockSpec((CHUNK, D_INNER), rowblk),
                  pl.BlockSpec((CONV_K, CONV_DIM), const),
                  pl.BlockSpec((1, CONV_DIM), const),
                  pl.BlockSpec((1, DT_PAD), const),
                  pl.BlockSpec((1, DT_PAD), const),
                  pl.BlockSpec((1, D_INNER), const),
                  pl.BlockSpec((1, D_INNER), const),
                  pl.BlockSpec((DEC_BATCH, D_INNER), const),
                  pl.BlockSpec((DEC_BATCH, D_INNER), const)],
        out_specs=[pl.BlockSpec((CHUNK, D_INNER), rowblk),
                   pl.BlockSpec((1, CONV_K - 1, CONV_DIM), lambda i: (seq(i), 0, 0)),
                   pl.BlockSpec((1, HEADS, HEADDIM, STATE), lambda i: (seq(i), 0, 0, 0))],
        out_shape=[jax.ShapeDtypeStruct((T_ALL, D_INNER), BF16),
                   jax.ShapeDtypeStruct((BATCH, CONV_K - 1, CONV_DIM), F32),
                   jax.ShapeDtypeStruct((BATCH, HEADS, HEADDIM, STATE), F32)],
        scratch_shapes=[pltpu.VMEM((8 + CHUNK, CONV_DIM), F32),
                        pltpu.VMEM((CHUNK, CONV_DIM), F32),
                        pltpu.VMEM((DT_PAD, D_INNER), BF16),
                        pltpu.VMEM((DT_PAD, HEADS * CHUNK), BF16),
                        pltpu.VMEM((CHUNK, HEADS * CHUNK), F32),
                        pltpu.VMEM((DT_PAD, CHUNK), F32),
                        pltpu.VMEM((DT_PAD, CHUNK), F32),
                        pltpu.VMEM((DT_PAD, CHUNK), F32),
                        pltpu.VMEM((HEAD_PAIRS, STATE, 128), F32),
                        pltpu.VMEM((CHUNK, D_INNER), F32)],
        compiler_params=_params("arbitrary"),
        name="ssd",
    )(xbc, dtr, zs, convw, convb, dtb, alog, dexp, normw, y_sample, xs_sample)


def _sgu_kernel(uvg_ref, lng_ref, lnb_ref, w_ref, bfull_ref, w0_ref, b0_ref, yb_ref, v_ref, vs_ref):
    i = pl.program_id(0)
    vn = _layer_norm(uvg_ref[:, D_GATE:2 * D_GATE], lng_ref[...], lnb_ref[...])

    @pl.when(i == BATCH * N_CHUNKS)
    def _():
        vs_ref[...] = vn
        s = w0_ref[...] * vn + b0_ref[...]
        yb_ref[...] = (uvg_ref[:, 0:D_GATE] * s).astype(yb_ref.dtype)

    @pl.when(i < BATCH * N_CHUNKS)
    def _():
        v_ref[0] = vn
        row = lax.broadcasted_iota(jnp.int32, (CHUNK, CHUNK), 0)
        col = lax.broadcasted_iota(jnp.int32, (CHUNK, CHUNK), 1)
        causal = row >= col
        for g in range(MLP_GROUPS):
            sl = slice(g * 128, (g + 1) * 128)
            wg = jnp.where(causal, w_ref[g], 0.0).astype(BF16)
            s = _dot(wg, vn[:, sl].astype(BF16)) + bfull_ref[:, sl]
            yb_ref[:, sl] = (uvg_ref[:, sl] * s).astype(yb_ref.dtype)


def _sgu(uvg, lng, lnb, w, bfull, w0, b0):
    const2 = lambda i: (0, 0)
    seq = lambda i: jnp.minimum(i // N_CHUNKS, BATCH - 1)
    return pl.pallas_call(
        _sgu_kernel,
        grid=(BATCH * N_CHUNKS + 1,),
        in_specs=[pl.BlockSpec((CHUNK, 2 * D_GATE), lambda i: (i, 0)),
                  pl.BlockSpec((1, D_GATE), const2),
                  pl.BlockSpec((1, D_GATE), const2),
                  pl.BlockSpec((MLP_GROUPS, CHUNK, CHUNK), lambda i: (0, 0, 0)),
                  pl.BlockSpec((CHUNK, D_GATE), const2),
                  pl.BlockSpec((1, D_GATE), const2),
                  pl.BlockSpec((1, D_GATE), const2)],
        out_specs=[pl.BlockSpec((CHUNK, D_GATE), lambda i: (i, 0)),
                   pl.BlockSpec((1, CHUNK, D_GATE), lambda i: (seq(i), 0, 0)),
                   pl.BlockSpec((DEC_BATCH, D_GATE), const2)],
        out_shape=[jax.ShapeDtypeStruct((T_ALL, D_GATE), BF16),
                   jax.ShapeDtypeStruct((BATCH, CHUNK, D_GATE), F32),
                   jax.ShapeDtypeStruct((DEC_BATCH, D_GATE), F32)],
        compiler_params=_params("arbitrary"),
        name="sgu",
    )(uvg, lng, lnb, w, bfull, w0, b0)


def _sample_pre_kernel(xbc_ref, sc_ref, dtr_ref, convw_ref, convb_ref, dtb_ref, alog_ref,
                       convnew_ref, xs_ref, xdt_ref, bm_ref, cm_ref, dec_ref):
    x = xbc_ref[...]
    acc = convb_ref[...] + convw_ref[3:4, :] * x
    acc = acc + convw_ref[2:3, :] * sc_ref[2]
    acc = acc + convw_ref[1:2, :] * sc_ref[1]
    acc = acc + convw_ref[0:1, :] * sc_ref[0]
    xc = _silu(acc)
    convnew_ref[0] = sc_ref[1]
    convnew_ref[1] = sc_ref[2]
    convnew_ref[2] = x
    dtv = _softplus(dtr_ref[...] + dtb_ref[...])
    dec_ref[...] = jnp.exp(dtv * (-jnp.exp(alog_ref[...])))
    hrow = lax.broadcasted_iota(jnp.int32, (DT_PAD, D_INNER), 0)
    col = lax.broadcasted_iota(jnp.int32, (DT_PAD, D_INNER), 1)
    rexp = jnp.where(lax.shift_right_logical(col, 6) == hrow, 1.0, 0.0).astype(F32)
    xs = xc[:, 0:D_INNER]
    xs_ref[...] = xs
    xdt_ref[...] = xs * _dot_f32(dtv, rexp)
    bm_ref[...] = xc[:, B_OFF:C_OFF]
    cm_ref[...] = xc[:, C_OFF:CONV_DIM]


def _sample_pre(xbc, sc_t, dtr, convw, convb, dtb, alog):
    blk = lambda i: (SAMPLE_BLOCK, 0)
    const = lambda i: (0, 0)
    return pl.pallas_call(
        _sample_pre_kernel,
        grid=(1,),
        in_specs=[pl.BlockSpec((DEC_BATCH, CONV_DIM), blk),
                  pl.BlockSpec((CONV_K - 1, DEC_BATCH, CONV_DIM), lambda i: (0, 0, 0)),
                  pl.BlockSpec((DEC_BATCH, DT_PAD), blk),
                  pl.BlockSpec((CONV_K, CONV_DIM), const),
                  pl.BlockSpec((1, CONV_DIM), const),
                  pl.BlockSpec((1, DT_PAD), const),
                  pl.BlockSpec((1, DT_PAD), const)],
        out_specs=[pl.BlockSpec((CONV_K - 1, DEC_BATCH, CONV_DIM), lambda i: (0, 0, 0)),
                   pl.BlockSpec((DEC_BATCH, D_INNER), const),
                   pl.BlockSpec((DEC_BATCH, D_INNER), const),
                   pl.BlockSpec((DEC_BATCH, GROUPS * STATE), const),
                   pl.BlockSpec((DEC_BATCH, GROUPS * STATE), const),
                   pl.BlockSpec((DEC_BATCH, DT_PAD), const)],
        out_shape=[jax.ShapeDtypeStruct((CONV_K - 1, DEC_BATCH, CONV_DIM), F32),
                   jax.ShapeDtypeStruct((DEC_BATCH, D_INNER), F32),
                   jax.ShapeDtypeStruct((DEC_BATCH, D_INNER), F32),
                   jax.ShapeDtypeStruct((DEC_BATCH, GROUPS * STATE), F32),
                   jax.ShapeDtypeStruct((DEC_BATCH, GROUPS * STATE), F32),
                   jax.ShapeDtypeStruct((DEC_BATCH, DT_PAD), F32)],
        compiler_params=_params("arbitrary"),
        name="sample_pre",
    )(xbc, sc_t, dtr, convw, convb, dtb, alog)


def _sample_state_kernel(dec_ref, xdtt_ref, bm_ref, cmt_ref, st_ref, new_ref, yt_ref):
    srow = lax.broadcasted_iota(jnp.int32, (DEC_BATCH, STATE), 0)
    slane = lax.broadcasted_iota(jnp.int32, (STATE, DEC_BATCH), 1)
    gw = HEADS_PER_GROUP * HEADDIM

    @pl.when(pl.program_id(0) == 0)
    def _():
        yt_ref[...] = jnp.zeros(yt_ref.shape, F32)

    for i in range(SAMPLES_PER_STEP):
        s = pl.program_id(0) * SAMPLES_PER_STEP + i
        brow = bm_ref[pl.ds(s, 1), :]
        for g in range(GROUPS):
            sel_b = jnp.where(srow == s, brow[:, g * STATE:(g + 1) * STATE], 0.0).astype(BF16)
            upd = _dot(xdtt_ref[g * gw:(g + 1) * gw, :], sel_b)
            news = []
            for r in range(HEADS_PER_GROUP):
                h = g * HEADS_PER_GROUP + r
                rows = slice(h * HEADDIM, (h + 1) * HEADDIM)
                new = st_ref[i, rows, :] * dec_ref[s, h] + upd[r * HEADDIM:(r + 1) * HEADDIM, :]
                new_ref[i, rows, :] = new
                news.append(new.astype(BF16))
            sel_c = jnp.where(slane == s, cmt_ref[g * STATE:(g + 1) * STATE, :], 0.0).astype(BF16)
            yt_ref[g * gw:(g + 1) * gw, :] += _dot(jnp.concatenate(news, axis=0), sel_c)


def _sample_state(dec, xdtt, bm, cmt, state):
    const = lambda s: (0, 0)
    blk = pl.BlockSpec((SAMPLES_PER_STEP, D_INNER, STATE), lambda s: (s, 0, 0))
    return pl.pallas_call(
        _sample_state_kernel,
        grid=(DEC_BATCH // SAMPLES_PER_STEP,),
        in_specs=[pl.BlockSpec(memory_space=pltpu.SMEM),
                  pl.BlockSpec((D_INNER, DEC_BATCH), const),
                  pl.BlockSpec((DEC_BATCH, GROUPS * STATE), const),
                  pl.BlockSpec((GROUPS * STATE, DEC_BATCH), const),
                  blk],
        out_specs=[blk, pl.BlockSpec((D_INNER, DEC_BATCH), const)],
        out_shape=[jax.ShapeDtypeStruct((DEC_BATCH, D_INNER, STATE), F32),
                   jax.ShapeDtypeStruct((D_INNER, DEC_BATCH), F32)],
        compiler_params=_params("arbitrary"),
        name="sample_state",
    )(dec, xdtt, bm, cmt, state)


def _merge_kernel(ya_ref, wa_ref, yb_ref, wb_ref, ga_ref, gb_ref, o_ref):
    o_ref[...] = (ga_ref[...] * _dot(ya_ref[...], wa_ref[...])
                  + gb_ref[...] * _dot(yb_ref[...], wb_ref[...])).astype(o_ref.dtype)


def _merge(ya, wa, yb, wb, gates):
    tm, tn = MM_TM, MM_TN // 2
    nb = D_MODEL // tn
    return pl.pallas_call(
        _merge_kernel,
        grid=(T_ALL // tm, nb),
        in_specs=[pl.BlockSpec((tm, D_INNER), lambda i, j: (i, 0)),
                  pl.BlockSpec((D_INNER, tn), lambda i, j: (0, j)),
                  pl.BlockSpec((tm, D_GATE), lambda i, j: (i, 0)),
                  pl.BlockSpec((D_GATE, tn), lambda i, j: (0, j)),
                  pl.BlockSpec((tm, tn), lambda i, j: (i, j)),
                  pl.BlockSpec((tm, tn), lambda i, j: (i, j + nb))],
        out_specs=pl.BlockSpec((tm, tn), lambda i, j: (i, j)),
        out_shape=jax.ShapeDtypeStruct((T_ALL, D_MODEL), BF16),
        compiler_params=_params("parallel", "parallel"),
        name="merge",
    )(ya, wa, yb, wb, gates, gates)


def _out_ln_kernel(m_ref, w_ref, x_ref, g_ref, b_ref, h_ref, hb_ref):
    h = _layer_norm(DN_ALPHA * x_ref[...] + _dot(m_ref[...], w_ref[...]), g_ref[...], b_ref[...])
    h_ref[...] = h
    hb_ref[...] = h.astype(BF16)


def _out_ln(merged, wout, x, g, b):
    tm = TOKEN_TILE
    const = lambda i: (0, 0)
    rows = lambda i: (i, 0)
    return pl.pallas_call(
        _out_ln_kernel,
        grid=(T_ALL // tm,),
        in_specs=[pl.BlockSpec((tm, D_MODEL), rows),
                  pl.BlockSpec((D_MODEL, D_MODEL), const),
                  pl.BlockSpec((tm, D_MODEL), rows),
                  pl.BlockSpec((1, D_MODEL), const),
                  pl.BlockSpec((1, D_MODEL), const)],
        out_specs=[pl.BlockSpec((tm, D_MODEL), rows), pl.BlockSpec((tm, D_MODEL), rows)],
        out_shape=[jax.ShapeDtypeStruct((T_ALL, D_MODEL), F32),
                   jax.ShapeDtypeStruct((T_ALL, D_MODEL), BF16)],
        compiler_params=_params("parallel"),
        name="out_ln",
    )(merged, wout, x, g, b)


def _final_ln_kernel(h_ref, pt_ref, g_ref, b_ref, o_ref):
    o_ref[...] = _layer_norm(DN_ALPHA * h_ref[...] + pt_ref[...].T, g_ref[...], b_ref[...])


def _final_ln(h, pt, g, b, *, tm, first_block, n_blocks, name):
    const = lambda i: (0, 0)
    rows = lambda i: (first_block + i, 0)
    return pl.pallas_call(
        _final_ln_kernel,
        grid=(n_blocks,),
        in_specs=[pl.BlockSpec((tm, D_MODEL), rows),
                  pl.BlockSpec((D_MODEL, tm), lambda i: (0, first_block + i)),
                  pl.BlockSpec((1, D_MODEL), const), pl.BlockSpec((1, D_MODEL), const)],
        out_specs=pl.BlockSpec((tm, D_MODEL), lambda i: (i, 0)),
        out_shape=jax.ShapeDtypeStruct((n_blocks * tm, D_MODEL), F32),
        compiler_params=_params("parallel"),
        name=name,
    )(h, pt, g, b)


def _compare_exchange(v, i, l, descending):
    hi, lo = jnp.maximum(v[i], v[l]), jnp.minimum(v[i], v[l])
    v[i], v[l] = (hi, lo) if descending else (lo, hi)


def _bitonic_merge_desc(v):
    v = list(v)
    j = len(v) // 2
    while j >= 1:
        for i in range(len(v)):
            if i ^ j > i:
                _compare_exchange(v, i, i ^ j, True)
        j //= 2
    return v


def _bitonic_sort_desc(v):
    v = list(v)
    k = 2
    while k <= len(v):
        j = k // 2
        while j >= 1:
            for i in range(len(v)):
                if i ^ j > i:
                    _compare_exchange(v, i, i ^ j, (i & k) == 0)
            j //= 2
        k *= 2
    return v


def _merge_sublanes_top16(v):
    for shift in (4, 6, 7):
        rolled = [pltpu.roll(a, shift, axis=0) for a in v]
        v = _bitonic_merge_desc([jnp.maximum(v[i], rolled[15 - i]) for i in range(16)])
    return [a[0:1, :] for a in v]


def _top16_of_rows(x):
    return _merge_sublanes_top16(_bitonic_sort_desc([x[8 * i:8 * i + 8, :] for i in range(16)]))


def _top16_of_list(rows):
    lanes = rows[0].shape[1]
    rows = rows + [jnp.full((1, lanes), NEG_INF, F32)] * (64 - len(rows))
    v = _bitonic_sort_desc([jnp.concatenate(rows[8 * i:8 * i + 8], axis=0) for i in range(8)])
    rolled = [pltpu.roll(a, 4, axis=0) for a in v]
    v = _bitonic_merge_desc(v + rolled[::-1])
    for shift in (6, 7):
        rolled = [pltpu.roll(a, shift, axis=0) for a in v]
        v = _bitonic_merge_desc([jnp.maximum(v[i], rolled[15 - i]) for i in range(16)])
    return [a[0:1, :] for a in v]


def _peer_front_kernel(qt_ref, keys_ref, s1_ref, e1_ref, s2_ref, e2_ref, thr_ref):
    k = PEER_TOPK
    for h in range(PEER_HEADS):
        sc = []
        top = []
        for side in range(2):
            hs = 2 * h + side
            x = _dot(keys_ref[hs], qt_ref[hs * PEER_KEYS:(hs + 1) * PEER_KEYS, :])
            sc.append(x)
            top.append(_top16_of_rows(x))
        best = _top16_of_list([top[0][i] + top[1][j] for i in range(k) for j in range(k)
                               if (i + 1) * (j + 1) <= k])
        z = jnp.zeros_like(best[0])
        for bk in best:
            z = z + jnp.exp(bk - best[0])
        s1_ref[h] = sc[0]
        s2_ref[h] = sc[1]
        e1_ref[h] = jnp.exp(sc[0] - top[0][0]) * (1.0 / z)
        e2_ref[h] = jnp.exp(sc[1] - top[1][0])
        thr_ref[h:h + 1, :] = best[k - 1]


def _peer_front(qt, keys):
    tt = 128
    blk3 = pl.BlockSpec((PEER_HEADS, PEER_KEYS, tt), lambda i: (0, 0, i))
    shp3 = jax.ShapeDtypeStruct((PEER_HEADS, PEER_KEYS, T_PAD), F32)
    return pl.pallas_call(
        _peer_front_kernel,
        grid=(T_PAD // tt,),
        in_specs=[pl.BlockSpec((2 * PEER_HEADS * PEER_KEYS, tt), lambda i: (0, i)),
                  pl.BlockSpec((2 * PEER_HEADS, PEER_KEYS, PEER_KEYS), lambda i: (0, 0, 0))],
        out_specs=[blk3, blk3, blk3, blk3, pl.BlockSpec((PEER_HEADS, tt), lambda i: (0, i))],
        out_shape=[shp3, shp3, shp3, shp3, jax.ShapeDtypeStruct((PEER_HEADS, T_PAD), F32)],
        compiler_params=_params("parallel"),
        name="peer_front",
    )(qt, keys)


def _peer_main_kernel(ht_ref, u_ref, vt_ref, s1_ref, e1_ref, s2_ref, e2_ref, thr_ref, o_ref, *act_s):
    j = pl.program_id(1)
    tt = ht_ref.shape[1]
    n_i1 = PEER_SUB // PEER_KEYS
    hk = PEER_GATE_ROWS
    n_chunks = tt // PEER_CHUNK
    chains = [(sub, c) for sub in range(PEER_EB // PEER_SUB) for c in range(n_chunks)]
    i1_rows = pl.ds(pl.multiple_of(j * (PEER_EB // PEER_KEYS), PEER_EB // PEER_KEYS), PEER_EB // PEER_KEYS)

    @pl.when(j == 0)
    def _():
        o_ref[...] = jnp.zeros(o_ref.shape, F32)

    slot = lax.rem(j, 2)

    def activations(k):
        sub, c = chains[k]
        act_s[k % 2][slot] = _dot(u_ref[sub * PEER_SUB:(sub + 1) * PEER_SUB, :],
                                  ht_ref[:, c * PEER_CHUNK:(c + 1) * PEER_CHUNK])
        return act_s[k % 2].at[slot]

    def gating(k, act, i1_list):
        sub, c = chains[k]
        tok_tiles = []
        for tc in range(PEER_CHUNK // 128):
            cols = slice(c * PEER_CHUNK + tc * 128, c * PEER_CHUNK + (tc + 1) * 128)
            lcols = slice(tc * 128, (tc + 1) * 128)
            parts = []
            for part in range(PEER_KEYS // hk):
                krows = slice(part * hk, (part + 1) * hk)
                gates = {r: jnp.zeros((hk, 128), F32) for r in i1_list}
                for h in range(PEER_HEADS):
                    s2t = s2_ref[h, krows, cols]
                    e2t = e2_ref[h, krows, cols]
                    thr = thr_ref[h:h + 1, cols]
                    s1t = s1_ref[h, i1_rows, cols]
                    e1t = e1_ref[h, i1_rows, cols]
                    for r in i1_list:
                        q = sub * n_i1 + r
                        pair = s1t[q:q + 1, :] + s2t
                        w = e1t[q:q + 1, :] * e2t
                        gates[r] = gates[r] + jnp.where(pair >= thr, w, 0.0)
                for r in i1_list:
                    arows = slice(r * PEER_KEYS + part * hk, r * PEER_KEYS + (part + 1) * hk)
                    gates[r] = gates[r] * _gelu(act[arows, lcols])
                parts.append(gates)
            tiles = [jnp.concatenate([p[r] for p in parts], axis=0).astype(BF16) for r in i1_list]
            tok_tiles.append(jnp.concatenate(tiles, axis=0))
        return jnp.concatenate(tok_tiles, axis=1)

    def accumulate(k, gt):
        sub, c = chains[k]
        o_ref[:, c * PEER_CHUNK:(c + 1) * PEER_CHUNK] += _dot(
            vt_ref[:, sub * PEER_SUB:(sub + 1) * PEER_SUB], gt)

    act = activations(0)
    for k in range(len(chains)):
        act_next = activations(k + 1) if k + 1 < len(chains) else None
        accumulate(k, gating(k, act, tuple(range(n_i1))))
        act = act_next


def _peer_main(ht, u, vt, s1, e1, s2, e2, thr):
    tt, eb = PEER_TT, PEER_EB
    once = pl.Buffered(1)
    blk3 = pl.BlockSpec((PEER_HEADS, PEER_KEYS, tt), lambda i, j: (0, 0, i), pipeline_mode=once)
    return pl.pallas_call(
        _peer_main_kernel,
        grid=(T_PAD // tt, PEER_EXPERTS // eb),
        in_specs=[pl.BlockSpec((D_MODEL, tt), lambda i, j: (0, i), pipeline_mode=once),
                  pl.BlockSpec((eb, D_MODEL), lambda i, j: (j, 0)),
                  pl.BlockSpec((D_MODEL, eb), lambda i, j: (0, j)),
                  blk3, blk3, blk3, blk3,
                  pl.BlockSpec((PEER_HEADS, tt), lambda i, j: (0, i), pipeline_mode=once)],
        out_specs=pl.BlockSpec((D_MODEL, tt), lambda i, j: (0, i)),
        out_shape=jax.ShapeDtypeStruct((D_MODEL, T_PAD), F32),
        scratch_shapes=[pltpu.VMEM((2, PEER_SUB, PEER_CHUNK), F32)] * 2,
        compiler_params=_params("parallel", "arbitrary"),
        name="peer_main",
    )(ht, u, vt, s1, e1, s2, e2, thr)


def kernel(x_prompt, x_sample, state_conv, state_ssm, w_in, conv_w, conv_b, dt_bias, a_log, d_skip,
           ssd_norm_w, sgu_ln_g, sgu_ln_b, sgu_w, sgu_b, w_branch_a, w_branch_b, w_out, ln1_g, ln1_b,
           peer_wq, peer_keys, peer_u, peer_v, ln2_g, ln2_b):
    row = lambda p: p[0].reshape(1, -1)
    x = jnp.concatenate([x_prompt.reshape(T_PROMPT, D_MODEL), x_sample.reshape(DEC_BATCH, D_MODEL)], axis=0)
    xb = x.astype(BF16)

    w = w_in[0]
    o1, o2, o3, o4 = D_INNER, D_INNER + CONV_DIM, D_INNER + CONV_DIM + HEADS, D_INNER + CONV_DIM + HEADS + 2 * D_GATE
    w_dt = jnp.pad(w[:, o2:o3], ((0, 0), (0, DT_PAD - HEADS)))
    mm = functools.partial(_matmul, tm=MM_TM)
    zs = mm(xb, w[:, :o1].astype(BF16), tn=MM_TN, act=_silu, name="proj_z")
    xbc = mm(xb, w[:, o1:o2].astype(BF16), tn=MM_TN, name="proj_xbc")
    dtr = mm(xb, w_dt.astype(BF16), tn=DT_PAD, name="proj_dt")
    uvg = mm(xb, w[:, o3:o4].astype(BF16), tn=MM_TN, act=_gelu, name="proj_uv")
    gates = mm(xb, w[:, o4:].astype(BF16), tn=MM_TN, act=jax.nn.sigmoid, name="proj_gates")

    pad_h = lambda p: jnp.pad(p[0], (0, DT_PAD - HEADS)).reshape(1, DT_PAD)
    dtb, alog = pad_h(dt_bias), pad_h(a_log)
    dexp = jnp.repeat(d_skip[0], HEADDIM).reshape(1, D_INNER)
    convw, convb, normw = conv_w[0], row(conv_b), row(ssd_norm_w)
    lng, lnb = row(sgu_ln_g), row(sgu_ln_b)

    sc_t = jnp.transpose(state_conv[0], (1, 0, 2))
    conv_s_t, xs_s, xdt_s, bm_s, cm_s, dec_s = _sample_pre(xbc, sc_t, dtr, convw, convb, dtb, alog)
    ssm_s, yt_s = _sample_state(dec_s, xdt_s.T.astype(BF16), bm_s, cm_s.T,
                                state_ssm[0].reshape(DEC_BATCH, D_INNER, STATE))

    ya, conv_p, ssm_p = _ssd(xbc, dtr, zs, convw, convb, dtb, alog, dexp, normw, yt_s.T, xs_s)
    bfull = jnp.repeat(sgu_b[0].T, CHUNK, axis=1)
    w0 = jnp.repeat(sgu_w[0][:, 0, 0], CHUNK).reshape(1, D_GATE)
    b0 = jnp.repeat(sgu_b[0][:, 0], CHUNK).reshape(1, D_GATE)
    yb, v_p, v_s = _sgu(uvg, lng, lnb, sgu_w[0], bfull, w0, b0)

    merged = _merge(ya, w_branch_a[0].astype(BF16), yb, w_branch_b[0].astype(BF16), gates)
    h, hb = _out_ln(merged, w_out[0].astype(BF16), x, row(ln1_g), row(ln1_b))

    ht = jnp.pad(hb.T, ((0, 0), (0, T_PAD - T_ALL)))
    qt = _matmul(peer_wq[0].T.astype(BF16), ht, tm=1024, tn=PEER_TT, out_dtype=BF16, name="peer_query")
    keys = peer_keys[0].reshape(2 * PEER_HEADS, PEER_KEYS, PEER_KEYS).astype(BF16)
    s1, e1, s2, e2, thr = _peer_front(qt, keys)
    p = _peer_main(ht, peer_u[0].astype(BF16), peer_v[0].T.astype(BF16), s1, e1, s2, e2, thr)
    y_prompt = _final_ln(h, p, row(ln2_g), row(ln2_b), tm=512, first_block=0, n_blocks=T_PROMPT // 512,
                         name="final_ln_prompt").reshape(BATCH, SEQ, D_MODEL)
    y_sample = _final_ln(h, p, row(ln2_g), row(ln2_b), tm=DEC_BATCH, first_block=SAMPLE_BLOCK, n_blocks=1,
                         name="final_ln_sample").reshape(DEC_BATCH, 1, D_MODEL)
    conv_sample = jnp.transpose(conv_s_t, (1, 0, 2))[None]
    return (y_prompt, y_sample, conv_p[None], ssm_p[None], v_p[None],
            conv_sample, ssm_s.reshape(1, DEC_BATCH, HEADS, HEADDIM, STATE), v_s[None, :, None, :])
```

```python
import functools
import math

import jax
import jax.numpy as jnp
from jax import lax
from jax.experimental import pallas as pl
from jax.experimental.pallas import tpu as pltpu

F32 = jnp.float32
BF16 = jnp.bfloat16

D_MODEL = 2048
SEQ = 2048
BATCH = 4
DEC_BATCH = 128
T_PROMPT = BATCH * SEQ
T_ALL = T_PROMPT + DEC_BATCH
CHUNK = 128
N_CHUNKS = SEQ // CHUNK
SAMPLE_BLOCK = T_PROMPT // CHUNK

D_INNER = 4096
HEADDIM = 64
HEADS = 64
GROUPS = 8
HEADS_PER_GROUP = HEADS // GROUPS
STATE = 128
CONV_K = 4
CONV_DIM = D_INNER + 2 * GROUPS * STATE
B_OFF = D_INNER
C_OFF = D_INNER + GROUPS * STATE
HEAD_PAIRS = HEADS // 2
DT_PAD = 128
D_GATE = 2048
MLP_GROUPS = 16
PEER_HEADS = 8
PEER_KEYS = 128
PEER_EXPERTS = PEER_KEYS * PEER_KEYS
PEER_TOPK = 16
DN_ALPHA = 2.0 ** 0.25
LN_EPS = 1e-5
RMS_EPS = 1e-5
NEG_INF = float("-inf")

MM_TM = 1040
MM_TN = 1024
SAMPLES_PER_STEP = 2
PEER_EB = 1024
PEER_SUB = 512
PEER_CHUNK = 256
PEER_GATE_ROWS = 64
PEER_TT = 768
T_PAD = 8448
VMEM_LIMIT = 56 * 1024 * 1024


def _params(*sem):
    return pltpu.CompilerParams(dimension_semantics=sem, vmem_limit_bytes=VMEM_LIMIT)


def _gelu(x):
    return 0.5 * x * (1.0 + lax.erf(x * (1.0 / math.sqrt(2.0))))


def _silu(x):
    return x * jax.nn.sigmoid(x)


def _identity(x):
    return x


def _softplus(x):
    return jnp.maximum(x, 0.0) + jnp.log1p(jnp.exp(-jnp.abs(x)))


def _layer_norm(x, g, b):
    mu = jnp.mean(x, axis=-1, keepdims=True)
    xc = x - mu
    var = jnp.mean(xc * xc, axis=-1, keepdims=True)
    return xc * lax.rsqrt(var + LN_EPS) * g + b


def _dot(a, b):
    return jnp.dot(a, b, preferred_element_type=F32)


def _dot_f32(a, b):
    return jnp.dot(a, b, preferred_element_type=F32, precision=lax.Precision.HIGHEST)


def _mm_kernel(a_ref, b_ref, o_ref, *, act):
    o_ref[...] = act(_dot(a_ref[...], b_ref[...])).astype(o_ref.dtype)


def _matmul(a, b, *, tm, tn, name, act=_identity, out_dtype=F32, first_col=0, n=None):
    m, k = a.shape
    n = b.shape[1] if n is None else n
    col0 = first_col // tn
    return pl.pallas_call(
        functools.partial(_mm_kernel, act=act),
        grid=(m // tm, n // tn),
        in_specs=[pl.BlockSpec((tm, k), lambda i, j: (i, 0)),
                  pl.BlockSpec((k, tn), lambda i, j: (0, col0 + j))],
        out_specs=pl.BlockSpec((tm, tn), lambda i, j: (i, j)),
        out_shape=jax.ShapeDtypeStruct((m, n), out_dtype),
        compiler_params=_params("parallel", "parallel"),
        name=name,
    )(a, b)


def _ssd_gate_norm(y, xs, zs, dexp, normw):
    yg = (y + dexp * xs) * zs
    gw = D_INNER // GROUPS
    outs = []
    for g in range(GROUPS):
        blk = yg[:, g * gw:(g + 1) * gw]
        ms = jnp.mean(blk * blk, axis=-1, keepdims=True)
        outs.append(blk * lax.rsqrt(ms + RMS_EPS) * normw[:, g * gw:(g + 1) * gw])
    return outs


def _dot_exact01(x, r01):
    hi = x.astype(BF16)
    rest = x - hi.astype(F32)
    mid = rest.astype(BF16)
    lo = (rest - mid.astype(F32)).astype(BF16)
    return (_dot(hi, r01) + _dot(mid, r01)) + _dot(lo, r01)


def _ssd_kernel(xbc_ref, dtr_ref, zs_ref, convw_ref, convb_ref, dtb_ref, alog_ref,
                dexp_ref, normw_ref, ys_ref, xss_ref,
                y_ref, convst_ref, ssm_ref,
                xext, xc_s, rexp_s, rcol_s, colb_s, acumt_s, dtt_s, wt_s, stt_s, ysc_s):
    i = pl.program_id(0)
    gw = D_INNER // GROUPS

    @pl.when(i == 0)
    def _():
        hrow = lax.broadcasted_iota(jnp.int32, (DT_PAD, D_INNER), 0)
        col = lax.broadcasted_iota(jnp.int32, (DT_PAD, D_INNER), 1)
        rexp_s[...] = jnp.where(lax.shift_right_logical(col, 6) == hrow, 1.0, 0.0).astype(BF16)
        hrow = lax.broadcasted_iota(jnp.int32, (DT_PAD, HEADS * CHUNK), 0)
        col = lax.broadcasted_iota(jnp.int32, (DT_PAD, HEADS * CHUNK), 1)
        rcol_s[...] = jnp.where(lax.shift_right_logical(col, 7) == hrow, 1.0, 0.0).astype(BF16)

    @pl.when(i == BATCH * N_CHUNKS)
    def _():
        outs = _ssd_gate_norm(ys_ref[...], xss_ref[...], zs_ref[...], dexp_ref[...], normw_ref[...])
        for g in range(GROUPS):
            y_ref[:, g * gw:(g + 1) * gw] = outs[g].astype(y_ref.dtype)

    @pl.when(i < BATCH * N_CHUNKS)
    def _():
        _ssd_prompt_chunk(i % N_CHUNKS, xbc_ref, dtr_ref, zs_ref, convw_ref, convb_ref, dtb_ref, alog_ref,
                          dexp_ref, normw_ref, y_ref, convst_ref, ssm_ref,
                          xext, xc_s, rexp_s, rcol_s, colb_s, acumt_s, dtt_s, wt_s, stt_s, ysc_s)


def _ssd_prompt_chunk(c, xbc_ref, dtr_ref, zs_ref, convw_ref, convb_ref, dtb_ref, alog_ref,
                      dexp_ref, normw_ref, y_ref, convst_ref, ssm_ref,
                      xext, xc_s, rexp_s, rcol_s, colb_s, acumt_s, dtt_s, wt_s, stt_s, ysc_s):
    gw = D_INNER // GROUPS

    @pl.when(c == 0)
    def _():
        xext[0:8, :] = jnp.zeros((8, CONV_DIM), F32)
        stt_s[...] = jnp.zeros(stt_s.shape, F32)

    xext[8:8 + CHUNK, :] = xbc_ref[...]
    slab = 512
    for s0 in range(0, CONV_DIM, slab):
        sl = slice(s0, s0 + slab)
        ext = xext[:, sl]
        acc = convw_ref[0:1, sl] * ext
        acc = convw_ref[1:2, sl] * ext + pltpu.roll(acc, 1, axis=0)
        acc = convw_ref[2:3, sl] * ext + pltpu.roll(acc, 1, axis=0)
        acc = convw_ref[3:4, sl] * ext + pltpu.roll(acc, 1, axis=0)
        xc_s[:, sl] = _silu(convb_ref[:, sl] + acc[8:8 + CHUNK, :])
    convst_ref[0] = xbc_ref[CHUNK - 3:CHUNK, :]
    xext[0:8, :] = xbc_ref[CHUNK - 8:CHUNK, :]

    dtv = _softplus(dtr_ref[...] + dtb_ref[...])
    a = dtv * (-jnp.exp(alog_ref[...]))
    row = lax.broadcasted_iota(jnp.int32, (CHUNK, CHUNK), 0)
    colq = lax.broadcasted_iota(jnp.int32, (CHUNK, CHUNK), 1)
    causal = row >= colq
    acum = _dot_f32(jnp.where(causal, 1.0, 0.0).astype(F32), a)
    last = acum[CHUNK - 1:CHUNK, :]
    acumt_s[...] = acum.T
    dtt_s[...] = dtv.T
    wt_s[...] = (dtv * jnp.exp(last - acum)).T
    colb_s[...] = _dot_exact01(acum, rcol_s[...])
    chunk_decay = jnp.exp(_dot_exact01(jnp.broadcast_to(last, (8, DT_PAD)), rexp_s[...]))[0:1, :]

    lane = lax.broadcasted_iota(jnp.int32, (CHUNK, 128), 1)
    lo = lane < HEADDIM
    zero_b = jnp.zeros((CHUNK, 128), BF16)
    for g in range(GROUPS):
        bg = xc_s[:, B_OFF + g * STATE:B_OFF + (g + 1) * STATE]
        cg = xc_s[:, C_OFF + g * STATE:C_OFF + (g + 1) * STATE]
        cb = lax.dot_general(cg.astype(BF16), bg.astype(BF16), (((1,), (1,)), ((), ())),
                             preferred_element_type=F32)
        bgt = bg.T
        for pr in range(HEADS_PER_GROUP // 2):
            hp = g * (HEADS_PER_GROUP // 2) + pr
            psl = slice(hp * 128, (hp + 1) * 128)
            heads = (2 * hp, 2 * hp + 1)
            lhs = []
            for hh in heads:
                colb = colb_s[:, hh * CHUNK:(hh + 1) * CHUNK]
                seg = colb - acumt_s[hh:hh + 1, :]
                decay = jnp.exp(jnp.where(causal, seg, NEG_INF))
                lhs.append((cb * decay * dtt_s[hh:hh + 1, :]).astype(BF16))
            for hh in heads:
                colb = colb_s[:, hh * CHUNK:(hh + 1) * CHUNK]
                lhs.append((cg * jnp.exp(colb)).astype(BF16))
            xpair = xc_s[:, psl].astype(BF16)
            spair = stt_s[hp].astype(BF16)
            x_lo, x_hi = jnp.where(lo, xpair, zero_b), jnp.where(lo, zero_b, xpair)
            rhs = jnp.concatenate([x_lo, x_hi, jnp.where(lo, spair, zero_b), jnp.where(lo, zero_b, spair)],
                                  axis=0)
            ysc_s[:, psl] = _dot(jnp.concatenate(lhs, axis=1), rhs)
            bw = jnp.concatenate([(bgt * wt_s[hh:hh + 1, :]).astype(BF16) for hh in heads], axis=1)
            upd = _dot(bw, jnp.concatenate([x_lo, x_hi], axis=0))
            stt_s[hp] = stt_s[hp] * chunk_decay[:, psl] + upd

    outs = _ssd_gate_norm(ysc_s[...], xc_s[:, 0:D_INNER], zs_ref[...], dexp_ref[...], normw_ref[...])
    for g in range(GROUPS):
        y_ref[:, g * gw:(g + 1) * gw] = outs[g].astype(y_ref.dtype)

    @pl.when(c == N_CHUNKS - 1)
    def _():
        for hp in range(HEAD_PAIRS):
            st = stt_s[hp].T
            ssm_ref[0, 2 * hp] = st[0:HEADDIM, :]
            ssm_ref[0, 2 * hp + 1] = st[HEADDIM:2 * HEADDIM, :]


def _ssd(xbc, dtr, zs, convw, convb, dtb, alog, dexp, normw, y_sample, xs_sample):
    rowblk = lambda i: (i, 0)
    const = lambda i: (0, 0)
    seq = lambda i: jnp.minimum(i // N_CHUNKS, BATCH - 1)
    return pl.pallas_call(
        _ssd_kernel,
        grid=(BATCH * N_CHUNKS + 1,),
        in_specs=[pl.BlockSpec((CHUNK, CONV_DIM), rowblk),
                  pl.BlockSpec((CHUNK, DT_PAD), rowblk),
                  pl.BlockSpec((CHUNK, D_INNER), rowblk),
                  pl.BlockSpec((CONV_K, CONV_DIM), const),
                  pl.BlockSpec((1, CONV_DIM), const),
                  pl.BlockSpec((1, DT_PAD), const),
                  pl.BlockSpec((1, DT_PAD), const),
                  pl.BlockSpec((1, D_INNER), const),
                  pl.BlockSpec((1, D_INNER), const),
                  pl.BlockSpec((DEC_BATCH, D_INNER), const),
                  pl.BlockSpec((DEC_BATCH, D_INNER), const)],
        out_specs=[pl.BlockSpec((CHUNK, D_INNER), rowblk),
                   pl.BlockSpec((1, CONV_K - 1, CONV_DIM), lambda i: (seq(i), 0, 0)),
                   pl.BlockSpec((1, HEADS, HEADDIM, STATE), lambda i: (seq(i), 0, 0, 0))],
        out_shape=[jax.ShapeDtypeStruct((T_ALL, D_INNER), BF16),
                   jax.ShapeDtypeStruct((BATCH, CONV_K - 1, CONV_DIM), F32),
                   jax.ShapeDtypeStruct((BATCH, HEADS, HEADDIM, STATE), F32)],
        scratch_shapes=[pltpu.VMEM((8 + CHUNK, CONV_DIM), F32),
                        pltpu.VMEM((CHUNK, CONV_DIM), F32),
                        pltpu.VMEM((DT_PAD, D_INNER), BF16),
                        pltpu.VMEM((DT_PAD, HEADS * CHUNK), BF16),
                        pltpu.VMEM((CHUNK, HEADS * CHUNK), F32),
                        pltpu.VMEM((DT_PAD, CHUNK), F32),
                        pltpu.VMEM((DT_PAD, CHUNK), F32),
                        pltpu.VMEM((DT_PAD, CHUNK), F32),
                        pltpu.VMEM((HEAD_PAIRS, STATE, 128), F32),
                        pltpu.VMEM((CHUNK, D_INNER), F32)],
        compiler_params=_params("arbitrary"),
        name="ssd",
    )(xbc, dtr, zs, convw, convb, dtb, alog, dexp, normw, y_sample, xs_sample)


def _sgu_kernel(uvg_ref, lng_ref, lnb_ref, w_ref, bfull_ref, w0_ref, b0_ref, yb_ref, v_ref, vs_ref):
    i = pl.program_id(0)
    vn = _layer_norm(uvg_ref[:, D_GATE:2 * D_GATE], lng_ref[...], lnb_ref[...])

    @pl.when(i == BATCH * N_CHUNKS)
    def _():
        vs_ref[...] = vn
        s = w0_ref[...] * vn + b0_ref[...]
        yb_ref[...] = (uvg_ref[:, 0:D_GATE] * s).astype(yb_ref.dtype)

    @pl.when(i < BATCH * N_CHUNKS)
    def _():
        v_ref[0] = vn
        row = lax.broadcasted_iota(jnp.int32, (CHUNK, CHUNK), 0)
        col = lax.broadcasted_iota(jnp.int32, (CHUNK, CHUNK), 1)
        causal = row >= col
        for g in range(MLP_GROUPS):
            sl = slice(g * 128, (g + 1) * 128)
            wg = jnp.where(causal, w_ref[g], 0.0).astype(BF16)
            s = _dot(wg, vn[:, sl].astype(BF16)) + bfull_ref[:, sl]
            yb_ref[:, sl] = (uvg_ref[:, sl] * s).astype(yb_ref.dtype)


def _sgu(uvg, lng, lnb, w, bfull, w0, b0):
    const2 = lambda i: (0, 0)
    seq = lambda i: jnp.minimum(i // N_CHUNKS, BATCH - 1)
    return pl.pallas_call(
        _sgu_kernel,
        grid=(BATCH * N_CHUNKS + 1,),
        in_specs=[pl.BlockSpec((CHUNK, 2 * D_GATE), lambda i: (i, 0)),
                  pl.BlockSpec((1, D_GATE), const2),
                  pl.BlockSpec((1, D_GATE), const2),
                  pl.BlockSpec((MLP_GROUPS, CHUNK, CHUNK), lambda i: (0, 0, 0)),
                  pl.BlockSpec((CHUNK, D_GATE), const2),
                  pl.BlockSpec((1, D_GATE), const2),
                  pl.BlockSpec((1, D_GATE), const2)],
        out_specs=[pl.BlockSpec((CHUNK, D_GATE), lambda i: (i, 0)),
                   pl.BlockSpec((1, CHUNK, D_GATE), lambda i: (seq(i), 0, 0)),
                   pl.BlockSpec((DEC_BATCH, D_GATE), const2)],
        out_shape=[jax.ShapeDtypeStruct((T_ALL, D_GATE), BF16),
                   jax.ShapeDtypeStruct((BATCH, CHUNK, D_GATE), F32),
                   jax.ShapeDtypeStruct((DEC_BATCH, D_GATE), F32)],
        compiler_params=_params("arbitrary"),
        name="sgu",
    )(uvg, lng, lnb, w, bfull, w0, b0)


def _sample_pre_kernel(xbc_ref, sc_ref, dtr_ref, convw_ref, convb_ref, dtb_ref, alog_ref,
                       convnew_ref, xs_ref, xdt_ref, bm_ref, cm_ref, dec_ref):
    x = xbc_ref[...]
    acc = convb_ref[...] + convw_ref[3:4, :] * x
    acc = acc + convw_ref[2:3, :] * sc_ref[2]
    acc = acc + convw_ref[1:2, :] * sc_ref[1]
    acc = acc + convw_ref[0:1, :] * sc_ref[0]
    xc = _silu(acc)
    convnew_ref[0] = sc_ref[1]
    convnew_ref[1] = sc_ref[2]
    convnew_ref[2] = x
    dtv = _softplus(dtr_ref[...] + dtb_ref[...])
    dec_ref[...] = jnp.exp(dtv * (-jnp.exp(alog_ref[...])))
    hrow = lax.broadcasted_iota(jnp.int32, (DT_PAD, D_INNER), 0)
    col = lax.broadcasted_iota(jnp.int32, (DT_PAD, D_INNER), 1)
    rexp = jnp.where(lax.shift_right_logical(col, 6) == hrow, 1.0, 0.0).astype(F32)
    xs = xc[:, 0:D_INNER]
    xs_ref[...] = xs
    xdt_ref[...] = xs * _dot_f32(dtv, rexp)
    bm_ref[...] = xc[:, B_OFF:C_OFF]
    cm_ref[...] = xc[:, C_OFF:CONV_DIM]


def _sample_pre(xbc, sc_t, dtr, convw, convb, dtb, alog):
    blk = lambda i: (SAMPLE_BLOCK, 0)
    const = lambda i: (0, 0)
    return pl.pallas_call(
        _sample_pre_kernel,
        grid=(1,),
        in_specs=[pl.BlockSpec((DEC_BATCH, CONV_DIM), blk),
                  pl.BlockSpec((CONV_K - 1, DEC_BATCH, CONV_DIM), lambda i: (0, 0, 0)),
                  pl.BlockSpec((DEC_BATCH, DT_PAD), blk),
                  pl.BlockSpec((CONV_K, CONV_DIM), const),
                  pl.BlockSpec((1, CONV_DIM), const),
                  pl.BlockSpec((1, DT_PAD), const),
                  pl.BlockSpec((1, DT_PAD), const)],
        out_specs=[pl.BlockSpec((CONV_K - 1, DEC_BATCH, CONV_DIM), lambda i: (0, 0, 0)),
                   pl.BlockSpec((DEC_BATCH, D_INNER), const),
                   pl.BlockSpec((DEC_BATCH, D_INNER), const),
                   pl.BlockSpec((DEC_BATCH, GROUPS * STATE), const),
                   pl.BlockSpec((DEC_BATCH, GROUPS * STATE), const),
                   pl.BlockSpec((DEC_BATCH, DT_PAD), const)],
        out_shape=[jax.ShapeDtypeStruct((CONV_K - 1, DEC_BATCH, CONV_DIM), F32),
                   jax.ShapeDtypeStruct((DEC_BATCH, D_INNER), F32),
                   jax.ShapeDtypeStruct((DEC_BATCH, D_INNER), F32),
                   jax.ShapeDtypeStruct((DEC_BATCH, GROUPS * STATE), F32),
                   jax.ShapeDtypeStruct((DEC_BATCH, GROUPS * STATE), F32),
                   jax.ShapeDtypeStruct((DEC_BATCH, DT_PAD), F32)],
        compiler_params=_params("arbitrary"),
        name="sample_pre",
    )(xbc, sc_t, dtr, convw, convb, dtb, alog)


def _sample_state_kernel(dec_ref, xdtt_ref, bm_ref, cmt_ref, st_ref, new_ref, yt_ref):
    srow = lax.broadcasted_iota(jnp.int32, (DEC_BATCH, STATE), 0)
    slane = lax.broadcasted_iota(jnp.int32, (STATE, DEC_BATCH), 1)
    gw = HEADS_PER_GROUP * HEADDIM

    @pl.when(pl.program_id(0) == 0)
    def _():
        yt_ref[...] = jnp.zeros(yt_ref.shape, F32)

    for i in range(SAMPLES_PER_STEP):
        s = pl.program_id(0) * SAMPLES_PER_STEP + i
        brow = bm_ref[pl.ds(s, 1), :]
        for g in range(GROUPS):
            sel_b = jnp.where(srow == s, brow[:, g * STATE:(g + 1) * STATE], 0.0).astype(BF16)
            upd = _dot(xdtt_ref[g * gw:(g + 1) * gw, :], sel_b)
            news = []
            for r in range(HEADS_PER_GROUP):
                h = g * HEADS_PER_GROUP + r
                rows = slice(h * HEADDIM, (h + 1) * HEADDIM)
                new = st_ref[i, rows, :] * dec_ref[s, h] + upd[r * HEADDIM:(r + 1) * HEADDIM, :]
                new_ref[i, rows, :] = new
                news.append(new.astype(BF16))
            sel_c = jnp.where(slane == s, cmt_ref[g * STATE:(g + 1) * STATE, :], 0.0).astype(BF16)
            yt_ref[g * gw:(g + 1) * gw, :] += _dot(jnp.concatenate(news, axis=0), sel_c)


def _sample_state(dec, xdtt, bm, cmt, state):
    const = lambda s: (0, 0)
    blk = pl.BlockSpec((SAMPLES_PER_STEP, D_INNER, STATE), lambda s: (s, 0, 0))
    return pl.pallas_call(
        _sample_state_kernel,
        grid=(DEC_BATCH // SAMPLES_PER_STEP,),
        in_specs=[pl.BlockSpec(memory_space=pltpu.SMEM),
                  pl.BlockSpec((D_INNER, DEC_BATCH), const),
                  pl.BlockSpec((DEC_BATCH, GROUPS * STATE), const),
                  pl.BlockSpec((GROUPS * STATE, DEC_BATCH), const),
                  blk],
        out_specs=[blk, pl.BlockSpec((D_INNER, DEC_BATCH), const)],
        out_shape=[jax.ShapeDtypeStruct((DEC_BATCH, D_INNER, STATE), F32),
                   jax.ShapeDtypeStruct((D_INNER, DEC_BATCH), F32)],
        compiler_params=_params("arbitrary"),
        name="sample_state",
    )(dec, xdtt, bm, cmt, state)


def _merge_kernel(ya_ref, wa_ref, yb_ref, wb_ref, ga_ref, gb_ref, o_ref):
    o_ref[...] = (ga_ref[...] * _dot(ya_ref[...], wa_ref[...])
                  + gb_ref[...] * _dot(yb_ref[...], wb_ref[...])).astype(o_ref.dtype)


def _merge(ya, wa, yb, wb, gates):
    tm, tn = MM_TM, MM_TN // 2
    nb = D_MODEL // tn
    return pl.pallas_call(
        _merge_kernel,
        grid=(T_ALL // tm, nb),
        in_specs=[pl.BlockSpec((tm, D_INNER), lambda i, j: (i, 0)),
                  pl.BlockSpec((D_INNER, tn), lambda i, j: (0, j)),
                  pl.BlockSpec((tm, D_GATE), lambda i, j: (i, 0)),
                  pl.BlockSpec((D_GATE, tn), lambda i, j: (0, j)),
                  pl.BlockSpec((tm, tn), lambda i, j: (i, j)),
                  pl.BlockSpec((tm, tn), lambda i, j: (i, j + nb))],
        out_specs=pl.BlockSpec((tm, tn), lambda i, j: (i, j)),
        out_shape=jax.ShapeDtypeStruct((T_ALL, D_MODEL), BF16),
        compiler_params=_params("parallel", "parallel"),
        name="merge",
    )(ya, wa, yb, wb, gates, gates)


def _out_ln_kernel(m_ref, w_ref, x_ref, g_ref, b_ref, h_ref, hb_ref):
    h = _layer_norm(DN_ALPHA * x_ref[...] + _dot(m_ref[...], w_ref[...]), g_ref[...], b_ref[...])
    h_ref[...] = h
    hb_ref[...] = h.astype(BF16)


def _out_ln(merged, wout, x, g, b, *, tm, first_block, name):
    n_blocks = x.shape[0] // tm
    const = lambda i: (0, 0)
    rows = lambda i: (i, 0)
    return pl.pallas_call(
        _out_ln_kernel,
        grid=(n_blocks,),
        in_specs=[pl.BlockSpec((tm, D_MODEL), lambda i: (first_block + i, 0)),
                  pl.BlockSpec((D_MODEL, D_MODEL), const),
                  pl.BlockSpec((tm, D_MODEL), rows),
                  pl.BlockSpec((1, D_MODEL), const),
                  pl.BlockSpec((1, D_MODEL), const)],
        out_specs=[pl.BlockSpec((tm, D_MODEL), rows), pl.BlockSpec((tm, D_MODEL), rows)],
        out_shape=[jax.ShapeDtypeStruct((n_blocks * tm, D_MODEL), F32),
                   jax.ShapeDtypeStruct((n_blocks * tm, D_MODEL), BF16)],
        compiler_params=_params("parallel"),
        name=name,
    )(merged, wout, x, g, b)


def _final_ln_kernel(h_ref, pt_ref, g_ref, b_ref, o_ref):
    o_ref[...] = _layer_norm(DN_ALPHA * h_ref[...] + pt_ref[...].T, g_ref[...], b_ref[...])


def _final_ln(h, pt, g, b, *, tm, first_block, name):
    n_blocks = h.shape[0] // tm
    const = lambda i: (0, 0)
    return pl.pallas_call(
        _final_ln_kernel,
        grid=(n_blocks,),
        in_specs=[pl.BlockSpec((tm, D_MODEL), lambda i: (i, 0)),
                  pl.BlockSpec((D_MODEL, tm), lambda i: (0, first_block + i)),
                  pl.BlockSpec((1, D_MODEL), const), pl.BlockSpec((1, D_MODEL), const)],
        out_specs=pl.BlockSpec((tm, D_MODEL), lambda i: (i, 0)),
        out_shape=jax.ShapeDtypeStruct((n_blocks * tm, D_MODEL), F32),
        compiler_params=_params("parallel"),
        name=name,
    )(h, pt, g, b)


def _compare_exchange(v, i, l, descending):
    hi, lo = jnp.maximum(v[i], v[l]), jnp.minimum(v[i], v[l])
    v[i], v[l] = (hi, lo) if descending else (lo, hi)


def _bitonic_merge_desc(v):
    v = list(v)
    j = len(v) // 2
    while j >= 1:
        for i in range(len(v)):
            if i ^ j > i:
                _compare_exchange(v, i, i ^ j, True)
        j //= 2
    return v


def _bitonic_sort_desc(v):
    v = list(v)
    k = 2
    while k <= len(v):
        j = k // 2
        while j >= 1:
            for i in range(len(v)):
                if i ^ j > i:
                    _compare_exchange(v, i, i ^ j, (i & k) == 0)
            j //= 2
        k *= 2
    return v


def _merge_sublanes_top16(v):
    for shift in (4, 6, 7):
        rolled = [pltpu.roll(a, shift, axis=0) for a in v]
        v = _bitonic_merge_desc([jnp.maximum(v[i], rolled[15 - i]) for i in range(16)])
    return [a[0:1, :] for a in v]


def _top16_of_rows(x):
    return _merge_sublanes_top16(_bitonic_sort_desc([x[8 * i:8 * i + 8, :] for i in range(16)]))


def _top16_of_list(rows):
    lanes = rows[0].shape[1]
    rows = rows + [jnp.full((1, lanes), NEG_INF, F32)] * (64 - len(rows))
    v = _bitonic_sort_desc([jnp.concatenate(rows[8 * i:8 * i + 8], axis=0) for i in range(8)])
    rolled = [pltpu.roll(a, 4, axis=0) for a in v]
    v = _bitonic_merge_desc(v + rolled[::-1])
    for shift in (6, 7):
        rolled = [pltpu.roll(a, shift, axis=0) for a in v]
        v = _bitonic_merge_desc([jnp.maximum(v[i], rolled[15 - i]) for i in range(16)])
    return [a[0:1, :] for a in v]


def _peer_front_kernel(qt_ref, keys_ref, s1_ref, e1_ref, s2_ref, e2_ref, thr_ref):
    k = PEER_TOPK
    for h in range(PEER_HEADS):
        sc = []
        top = []
        for side in range(2):
            hs = 2 * h + side
            x = _dot(keys_ref[hs], qt_ref[hs * PEER_KEYS:(hs + 1) * PEER_KEYS, :])
            sc.append(x)
            top.append(_top16_of_rows(x))
        best = _top16_of_list([top[0][i] + top[1][j] for i in range(k) for j in range(k)
                               if (i + 1) * (j + 1) <= k])
        z = jnp.zeros_like(best[0])
        for bk in best:
            z = z + jnp.exp(bk - best[0])
        s1_ref[h] = sc[0]
        s2_ref[h] = sc[1]
        e1_ref[h] = jnp.exp(sc[0] - top[0][0]) * (1.0 / z)
        e2_ref[h] = jnp.exp(sc[1] - top[1][0])
        thr_ref[h:h + 1, :] = best[k - 1]


def _peer_front(qt, keys):
    tt = 128
    blk3 = pl.BlockSpec((PEER_HEADS, PEER_KEYS, tt), lambda i: (0, 0, i))
    shp3 = jax.ShapeDtypeStruct((PEER_HEADS, PEER_KEYS, T_PAD), F32)
    return pl.pallas_call(
        _peer_front_kernel,
        grid=(T_PAD // tt,),
        in_specs=[pl.BlockSpec((2 * PEER_HEADS * PEER_KEYS, tt), lambda i: (0, i)),
                  pl.BlockSpec((2 * PEER_HEADS, PEER_KEYS, PEER_KEYS), lambda i: (0, 0, 0))],
        out_specs=[blk3, blk3, blk3, blk3, pl.BlockSpec((PEER_HEADS, tt), lambda i: (0, i))],
        out_shape=[shp3, shp3, shp3, shp3, jax.ShapeDtypeStruct((PEER_HEADS, T_PAD), F32)],
        compiler_params=_params("parallel"),
        name="peer_front",
    )(qt, keys)


def _peer_main_kernel(ht_ref, u_ref, vt_ref, s1_ref, e1_ref, s2_ref, e2_ref, thr_ref, o_ref, *act_s):
    j = pl.program_id(1)
    tt = ht_ref.shape[1]
    n_i1 = PEER_SUB // PEER_KEYS
    hk = PEER_GATE_ROWS
    n_chunks = tt // PEER_CHUNK
    chains = [(sub, c) for sub in range(PEER_EB // PEER_SUB) for c in range(n_chunks)]
    i1_rows = pl.ds(pl.multiple_of(j * (PEER_EB // PEER_KEYS), PEER_EB // PEER_KEYS), PEER_EB // PEER_KEYS)

    @pl.when(j == 0)
    def _():
        o_ref[...] = jnp.zeros(o_ref.shape, F32)

    slot = lax.rem(j, 2)

    def activations(k):
        sub, c = chains[k]
        act_s[k % 2][slot] = _dot(u_ref[sub * PEER_SUB:(sub + 1) * PEER_SUB, :],
                                  ht_ref[:, c * PEER_CHUNK:(c + 1) * PEER_CHUNK])
        return act_s[k % 2].at[slot]

    def gating(k, act, i1_list):
        sub, c = chains[k]
        tok_tiles = []
        for tc in range(PEER_CHUNK // 128):
            cols = slice(c * PEER_CHUNK + tc * 128, c * PEER_CHUNK + (tc + 1) * 128)
            lcols = slice(tc * 128, (tc + 1) * 128)
            parts = []
            for part in range(PEER_KEYS // hk):
                krows = slice(part * hk, (part + 1) * hk)
                gates = {r: jnp.zeros((hk, 128), F32) for r in i1_list}
                for h in range(PEER_HEADS):
                    s2t = s2_ref[h, krows, cols]
                    e2t = e2_ref[h, krows, cols]
                    thr = thr_ref[h:h + 1, cols]
                    s1t = s1_ref[h, i1_rows, cols]
                    e1t = e1_ref[h, i1_rows, cols]
                    for r in i1_list:
                        q = sub * n_i1 + r
                        pair = s1t[q:q + 1, :] + s2t
                        w = e1t[q:q + 1, :] * e2t
                        gates[r] = gates[r] + jnp.where(pair >= thr, w, 0.0)
                for r in i1_list:
                    arows = slice(r * PEER_KEYS + part * hk, r * PEER_KEYS + (part + 1) * hk)
                    gates[r] = gates[r] * _gelu(act[arows, lcols])
                parts.append(gates)
            tiles = [jnp.concatenate([p[r] for p in parts], axis=0).astype(BF16) for r in i1_list]
            tok_tiles.append(jnp.concatenate(tiles, axis=0))
        return jnp.concatenate(tok_tiles, axis=1)

    def accumulate(k, gt):
        sub, c = chains[k]
        o_ref[:, c * PEER_CHUNK:(c + 1) * PEER_CHUNK] += _dot(
            vt_ref[:, sub * PEER_SUB:(sub + 1) * PEER_SUB], gt)

    act = activations(0)
    for k in range(len(chains)):
        act_next = activations(k + 1) if k + 1 < len(chains) else None
        accumulate(k, gating(k, act, tuple(range(n_i1))))
        act = act_next


def _peer_main(ht, u, vt, s1, e1, s2, e2, thr):
    tt, eb = PEER_TT, PEER_EB
    once = pl.Buffered(1)
    blk3 = pl.BlockSpec((PEER_HEADS, PEER_KEYS, tt), lambda i, j: (0, 0, i), pipeline_mode=once)
    return pl.pallas_call(
        _peer_main_kernel,
        grid=(T_PAD // tt, PEER_EXPERTS // eb),
        in_specs=[pl.BlockSpec((D_MODEL, tt), lambda i, j: (0, i), pipeline_mode=once),
                  pl.BlockSpec((eb, D_MODEL), lambda i, j: (j, 0)),
                  pl.BlockSpec((D_MODEL, eb), lambda i, j: (0, j)),
                  blk3, blk3, blk3, blk3,
                  pl.BlockSpec((PEER_HEADS, tt), lambda i, j: (0, i), pipeline_mode=once)],
        out_specs=pl.BlockSpec((D_MODEL, tt), lambda i, j: (0, i)),
        out_shape=jax.ShapeDtypeStruct((D_MODEL, T_PAD), F32),
        scratch_shapes=[pltpu.VMEM((2, PEER_SUB, PEER_CHUNK), F32)] * 2,
        compiler_params=_params("parallel", "arbitrary"),
        name="peer_main",
    )(ht, u, vt, s1, e1, s2, e2, thr)


def kernel(x_prompt, x_sample, state_conv, state_ssm, w_in, conv_w, conv_b, dt_bias, a_log, d_skip,
           ssd_norm_w, sgu_ln_g, sgu_ln_b, sgu_w, sgu_b, w_branch_a, w_branch_b, w_out, ln1_g, ln1_b,
           peer_wq, peer_keys, peer_u, peer_v, ln2_g, ln2_b):
    row = lambda p: p[0].reshape(1, -1)
    x_p, x_s = x_prompt.reshape(T_PROMPT, D_MODEL), x_sample.reshape(DEC_BATCH, D_MODEL)
    xb = jnp.concatenate([x_p.astype(BF16), x_s.astype(BF16)], axis=0)

    w = w_in[0]
    o2, o3 = D_INNER + CONV_DIM, D_INNER + CONV_DIM + HEADS
    wb = jnp.concatenate([w[:, :o2], jnp.pad(w[:, o2:o3], ((0, 0), (0, MM_TN - HEADS))), w[:, o3:]],
                         axis=1).astype(BF16)
    mm = functools.partial(_matmul, xb, wb, tm=MM_TM)
    zs = mm(tn=MM_TN, first_col=0, n=D_INNER, act=_silu, name="proj_z")
    xbc = mm(tn=MM_TN, first_col=D_INNER, n=CONV_DIM, name="proj_xbc")
    dtr = mm(tn=DT_PAD, first_col=o2, n=DT_PAD, name="proj_dt")
    uvg = mm(tn=MM_TN, first_col=o2 + MM_TN, n=2 * D_GATE, act=_gelu, name="proj_uv")
    gates = mm(tn=MM_TN, first_col=o2 + MM_TN + 2 * D_GATE, n=2 * D_MODEL, act=jax.nn.sigmoid,
               name="proj_gates")

    pad_h = lambda p: jnp.pad(p[0], (0, DT_PAD - HEADS)).reshape(1, DT_PAD)
    dtb, alog = pad_h(dt_bias), pad_h(a_log)
    dexp = jnp.repeat(d_skip[0], HEADDIM).reshape(1, D_INNER)
    convw, convb, normw = conv_w[0], row(conv_b), row(ssd_norm_w)
    lng, lnb = row(sgu_ln_g), row(sgu_ln_b)

    sc_t = jnp.transpose(state_conv[0], (1, 0, 2))
    conv_s_t, xs_s, xdt_s, bm_s, cm_s, dec_s = _sample_pre(xbc, sc_t, dtr, convw, convb, dtb, alog)
    ssm_s, yt_s = _sample_state(dec_s, xdt_s.T.astype(BF16), bm_s, cm_s.T,
                                state_ssm[0].reshape(DEC_BATCH, D_INNER, STATE))

    ya, conv_p, ssm_p = _ssd(xbc, dtr, zs, convw, convb, dtb, alog, dexp, normw, yt_s.T, xs_s)
    bfull = jnp.repeat(sgu_b[0].T, CHUNK, axis=1)
    w0 = jnp.repeat(sgu_w[0][:, 0, 0], CHUNK).reshape(1, D_GATE)
    b0 = jnp.repeat(sgu_b[0][:, 0], CHUNK).reshape(1, D_GATE)
    yb, v_p, v_s = _sgu(uvg, lng, lnb, sgu_w[0], bfull, w0, b0)

    merged = _merge(ya, w_branch_a[0].astype(BF16), yb, w_branch_b[0].astype(BF16), gates)
    wout, g1, b1 = w_out[0].astype(BF16), row(ln1_g), row(ln1_b)
    h_p, hb_p = _out_ln(merged, wout, x_p, g1, b1, tm=512, first_block=0, name="out_ln_prompt")
    h_s, hb_s = _out_ln(merged, wout, x_s, g1, b1, tm=DEC_BATCH, first_block=SAMPLE_BLOCK, name="out_ln_sample")

    ht = jnp.concatenate([hb_p.T, hb_s.T, jnp.zeros((D_MODEL, T_PAD - T_ALL), BF16)], axis=1)
    qt = _matmul(peer_wq[0].T.astype(BF16), ht, tm=1024, tn=PEER_TT, out_dtype=BF16, name="peer_query")
    keys = peer_keys[0].reshape(2 * PEER_HEADS, PEER_KEYS, PEER_KEYS).astype(BF16)
    s1, e1, s2, e2, thr = _peer_front(qt, keys)
    p = _peer_main(ht, peer_u[0].astype(BF16), peer_v[0].T.astype(BF16), s1, e1, s2, e2, thr)
    y_prompt = _final_ln(h_p, p, row(ln2_g), row(ln2_b), tm=512, first_block=0,
                         name="final_ln_prompt").reshape(BATCH, SEQ, D_MODEL)
    y_sample = _final_ln(h_s, p, row(ln2_g), row(ln2_b), tm=DEC_BATCH, first_block=SAMPLE_BLOCK,
                         name="final_ln_sample").reshape(DEC_BATCH, 1, D_MODEL)
    conv_sample = jnp.transpose(conv_s_t, (1, 0, 2))[None]
    return (y_prompt, y_sample, conv_p[None], ssm_p[None], v_p[None],
            conv_sample, ssm_s.reshape(1, DEC_BATCH, HEADS, HEADDIM, STATE), v_s[None, :, None, :])
```

```python
import functools
import math

import jax
import jax.numpy as jnp
from jax import lax
from jax.experimental import pallas as pl
from jax.experimental.pallas import tpu as pltpu

F32 = jnp.float32
BF16 = jnp.bfloat16

D_MODEL = 2048
SEQ = 2048
BATCH = 4
DEC_BATCH = 128
T_PROMPT = BATCH * SEQ
T_ALL = T_PROMPT + DEC_BATCH
CHUNK = 128
N_CHUNKS = SEQ // CHUNK
SAMPLE_BLOCK = T_PROMPT // CHUNK

D_INNER = 4096
HEADDIM = 64
HEADS = 64
GROUPS = 8
HEADS_PER_GROUP = HEADS // GROUPS
STATE = 128
CONV_K = 4
CONV_DIM = D_INNER + 2 * GROUPS * STATE
B_OFF = D_INNER
C_OFF = D_INNER + GROUPS * STATE
HEAD_PAIRS = HEADS // 2
DT_PAD = 128
D_GATE = 2048
MLP_GROUPS = 16
PEER_HEADS = 8
PEER_KEYS = 128
PEER_EXPERTS = PEER_KEYS * PEER_KEYS
PEER_TOPK = 16
DN_ALPHA = 2.0 ** 0.25
LN_EPS = 1e-5
RMS_EPS = 1e-5
NEG_INF = float("-inf")

OUT_LN_TM = 640
MM_TM = 1040
MM_TN = 1024
SAMPLES_PER_STEP = 2
PEER_EB = 2048
PEER_SUB = 512
PEER_CHUNK = 256
PEER_GATE_ROWS = 64
PEER_TT = 768
T_PAD = 8448
VMEM_LIMIT = 56 * 1024 * 1024


def _params(*sem):
    return pltpu.CompilerParams(dimension_semantics=sem, vmem_limit_bytes=VMEM_LIMIT)


def _gelu(x):
    return 0.5 * x * (1.0 + lax.erf(x * (1.0 / math.sqrt(2.0))))


def _silu(x):
    return x * jax.nn.sigmoid(x)


def _identity(x):
    return x


def _softplus(x):
    return jnp.maximum(x, 0.0) + jnp.log1p(jnp.exp(-jnp.abs(x)))


def _layer_norm(x, g, b):
    mu = jnp.mean(x, axis=-1, keepdims=True)
    xc = x - mu
    var = jnp.mean(xc * xc, axis=-1, keepdims=True)
    return xc * lax.rsqrt(var + LN_EPS) * g + b


def _dot(a, b):
    return jnp.dot(a, b, preferred_element_type=F32)


def _dot_f32(a, b):
    return jnp.dot(a, b, preferred_element_type=F32, precision=lax.Precision.HIGHEST)


def _mm_kernel(a_ref, b_ref, o_ref, *, act):
    o_ref[...] = act(_dot(a_ref[...], b_ref[...])).astype(o_ref.dtype)


def _matmul(a, b, *, tm, tn, name, act=_identity, out_dtype=F32):
    m, k = a.shape
    n = b.shape[1]
    return pl.pallas_call(
        functools.partial(_mm_kernel, act=act),
        grid=(m // tm, n // tn),
        in_specs=[pl.BlockSpec((tm, k), lambda i, j: (i, 0)),
                  pl.BlockSpec((k, tn), lambda i, j: (0, j))],
        out_specs=pl.BlockSpec((tm, tn), lambda i, j: (i, j)),
        out_shape=jax.ShapeDtypeStruct((m, n), out_dtype),
        compiler_params=_params("parallel", "parallel"),
        name=name,
    )(a, b)


def _ssd_gate_norm(y, xs, zs, dexp, normw):
    yg = (y + dexp * xs) * zs
    gw = D_INNER // GROUPS
    outs = []
    for g in range(GROUPS):
        blk = yg[:, g * gw:(g + 1) * gw]
        ms = jnp.mean(blk * blk, axis=-1, keepdims=True)
        outs.append(blk * lax.rsqrt(ms + RMS_EPS) * normw[:, g * gw:(g + 1) * gw])
    return outs


def _dot_exact01(x, r01):
    hi = x.astype(BF16)
    rest = x - hi.astype(F32)
    mid = rest.astype(BF16)
    lo = (rest - mid.astype(F32)).astype(BF16)
    return (_dot(hi, r01) + _dot(mid, r01)) + _dot(lo, r01)


def _ssd_kernel(xbc_ref, dtr_ref, zs_ref, convw_ref, convb_ref, dtb_ref, alog_ref,
                dexp_ref, normw_ref, ys_ref, xss_ref,
                y_ref, convst_ref, ssm_ref,
                xext, xc_s, rexp_s, rcol_s, colb_s, acumt_s, dtt_s, wt_s, stt_s, ysc_s):
    i = pl.program_id(0)
    gw = D_INNER // GROUPS

    @pl.when(i == 0)
    def _():
        hrow = lax.broadcasted_iota(jnp.int32, (DT_PAD, D_INNER), 0)
        col = lax.broadcasted_iota(jnp.int32, (DT_PAD, D_INNER), 1)
        rexp_s[...] = jnp.where(lax.shift_right_logical(col, 6) == hrow, 1.0, 0.0).astype(BF16)
        hrow = lax.broadcasted_iota(jnp.int32, (DT_PAD, HEADS * CHUNK), 0)
        col = lax.broadcasted_iota(jnp.int32, (DT_PAD, HEADS * CHUNK), 1)
        rcol_s[...] = jnp.where(lax.shift_right_logical(col, 7) == hrow, 1.0, 0.0).astype(BF16)

    @pl.when(i == BATCH * N_CHUNKS)
    def _():
        outs = _ssd_gate_norm(ys_ref[...], xss_ref[...], zs_ref[...], dexp_ref[...], normw_ref[...])
        for g in range(GROUPS):
            y_ref[:, g * gw:(g + 1) * gw] = outs[g].astype(y_ref.dtype)

    @pl.when(i < BATCH * N_CHUNKS)
    def _():
        _ssd_prompt_chunk(i % N_CHUNKS, xbc_ref, dtr_ref, zs_ref, convw_ref, convb_ref, dtb_ref, alog_ref,
                          dexp_ref, normw_ref, y_ref, convst_ref, ssm_ref,
                          xext, xc_s, rexp_s, rcol_s, colb_s, acumt_s, dtt_s, wt_s, stt_s, ysc_s)


def _ssd_prompt_chunk(c, xbc_ref, dtr_ref, zs_ref, convw_ref, convb_ref, dtb_ref, alog_ref,
                      dexp_ref, normw_ref, y_ref, convst_ref, ssm_ref,
                      xext, xc_s, rexp_s, rcol_s, colb_s, acumt_s, dtt_s, wt_s, stt_s, ysc_s):
    gw = D_INNER // GROUPS

    @pl.when(c == 0)
    def _():
        xext[0:8, :] = jnp.zeros((8, CONV_DIM), F32)
        stt_s[...] = jnp.zeros(stt_s.shape, F32)

    xext[8:8 + CHUNK, :] = xbc_ref[...]
    slab = 512
    for s0 in range(0, CONV_DIM, slab):
        sl = slice(s0, s0 + slab)
        ext = xext[:, sl]
        acc = convw_ref[0:1, sl] * ext
        acc = convw_ref[1:2, sl] * ext + pltpu.roll(acc, 1, axis=0)
        acc = convw_ref[2:3, sl] * ext + pltpu.roll(acc, 1, axis=0)
        acc = convw_ref[3:4, sl] * ext + pltpu.roll(acc, 1, axis=0)
        xc_s[:, sl] = _silu(convb_ref[:, sl] + acc[8:8 + CHUNK, :])
    convst_ref[0] = xbc_ref[CHUNK - 3:CHUNK, :]
    xext[0:8, :] = xbc_ref[CHUNK - 8:CHUNK, :]

    dtv = _softplus(dtr_ref[...] + dtb_ref[...])
    a = dtv * (-jnp.exp(alog_ref[...]))
    row = lax.broadcasted_iota(jnp.int32, (CHUNK, CHUNK), 0)
    colq = lax.broadcasted_iota(jnp.int32, (CHUNK, CHUNK), 1)
    causal = row >= colq
    acum = _dot_f32(jnp.where(causal, 1.0, 0.0).astype(F32), a)
    last = acum[CHUNK - 1:CHUNK, :]
    acumt_s[...] = acum.T
    dtt_s[...] = dtv.T
    wt_s[...] = (dtv * jnp.exp(last - acum)).T
    colb_s[...] = _dot_exact01(acum, rcol_s[...])
    chunk_decay = jnp.exp(_dot_exact01(jnp.broadcast_to(last, (8, DT_PAD)), rexp_s[...]))[0:1, :]

    lane = lax.broadcasted_iota(jnp.int32, (CHUNK, 128), 1)
    lo = lane < HEADDIM
    zero_b = jnp.zeros((CHUNK, 128), BF16)
    for g in range(GROUPS):
        bg = xc_s[:, B_OFF + g * STATE:B_OFF + (g + 1) * STATE]
        cg = xc_s[:, C_OFF + g * STATE:C_OFF + (g + 1) * STATE]
        cb = lax.dot_general(cg.astype(BF16), bg.astype(BF16), (((1,), (1,)), ((), ())),
                             preferred_element_type=F32)
        bgt = bg.T
        for pr in range(HEADS_PER_GROUP // 2):
            hp = g * (HEADS_PER_GROUP // 2) + pr
            psl = slice(hp * 128, (hp + 1) * 128)
            heads = (2 * hp, 2 * hp + 1)
            lhs = []
            for hh in heads:
                colb = colb_s[:, hh * CHUNK:(hh + 1) * CHUNK]
                seg = colb - acumt_s[hh:hh + 1, :]
                decay = jnp.exp(jnp.where(causal, seg, NEG_INF))
                lhs.append((cb * decay * dtt_s[hh:hh + 1, :]).astype(BF16))
            for hh in heads:
                colb = colb_s[:, hh * CHUNK:(hh + 1) * CHUNK]
                lhs.append((cg * jnp.exp(colb)).astype(BF16))
            xpair = xc_s[:, psl].astype(BF16)
            spair = stt_s[hp].astype(BF16)
            x_lo, x_hi = jnp.where(lo, xpair, zero_b), jnp.where(lo, zero_b, xpair)
            rhs = jnp.concatenate([x_lo, x_hi, jnp.where(lo, spair, zero_b), jnp.where(lo, zero_b, spair)],
                                  axis=0)
            ysc_s[:, psl] = _dot(jnp.concatenate(lhs, axis=1), rhs)
            bw = jnp.concatenate([(bgt * wt_s[hh:hh + 1, :]).astype(BF16) for hh in heads], axis=1)
            upd = _dot(bw, jnp.concatenate([x_lo, x_hi], axis=0))
            stt_s[hp] = stt_s[hp] * chunk_decay[:, psl] + upd

    outs = _ssd_gate_norm(ysc_s[...], xc_s[:, 0:D_INNER], zs_ref[...], dexp_ref[...], normw_ref[...])
    for g in range(GROUPS):
        y_ref[:, g * gw:(g + 1) * gw] = outs[g].astype(y_ref.dtype)

    @pl.when(c == N_CHUNKS - 1)
    def _():
        for hp in range(HEAD_PAIRS):
            st = stt_s[hp].T
            ssm_ref[0, 2 * hp] = st[0:HEADDIM, :]
            ssm_ref[0, 2 * hp + 1] = st[HEADDIM:2 * HEADDIM, :]


def _ssd(xbc, dtr, zs, convw, convb, dtb, alog, dexp, normw, y_sample, xs_sample):
    rowblk = lambda i: (i, 0)
    const = lambda i: (0, 0)
    seq = lambda i: jnp.minimum(i // N_CHUNKS, BATCH - 1)
    return pl.pallas_call(
        _ssd_kernel,
        grid=(BATCH * N_CHUNKS + 1,),
        in_specs=[pl.BlockSpec((CHUNK, CONV_DIM), rowblk),
                  pl.BlockSpec((CHUNK, DT_PAD), rowblk),
                  pl.BlockSpec((CHUNK, D_INNER), rowblk),
                  pl.BlockSpec((CONV_K, CONV_DIM), const),
                  pl.BlockSpec((1, CONV_DIM), const),
                  pl.BlockSpec((1, DT_PAD), const),
                  pl.BlockSpec((1, DT_PAD), const),
                  pl.BlockSpec((1, D_INNER), const),
                  pl.BlockSpec((1, D_INNER), const),
                  pl.BlockSpec((DEC_BATCH, D_INNER), const),
                  pl.BlockSpec((DEC_BATCH, D_INNER), const)],
        out_specs=[pl.BlockSpec((CHUNK, D_INNER), rowblk),
                   pl.BlockSpec((1, CONV_K - 1, CONV_DIM), lambda i: (seq(i), 0, 0)),
                   pl.BlockSpec((1, HEADS, HEADDIM, STATE), lambda i: (seq(i), 0, 0, 0))],
        out_shape=[jax.ShapeDtypeStruct((T_ALL, D_INNER), BF16),
                   jax.ShapeDtypeStruct((BATCH, CONV_K - 1, CONV_DIM), F32),
                   jax.ShapeDtypeStruct((BATCH, HEADS, HEADDIM, STATE), F32)],
        scratch_shapes=[pltpu.VMEM((8 + CHUNK, CONV_DIM), F32),
                        pltpu.VMEM((CHUNK, CONV_DIM), F32),
                        pltpu.VMEM((DT_PAD, D_INNER), BF16),
                        pltpu.VMEM((DT_PAD, HEADS * CHUNK), BF16),
                        pltpu.VMEM((CHUNK, HEADS * CHUNK), F32),
                        pltpu.VMEM((DT_PAD, CHUNK), F32),
                        pltpu.VMEM((DT_PAD, CHUNK), F32),
                        pltpu.VMEM((DT_PAD, CHUNK), F32),
                        pltpu.VMEM((HEAD_PAIRS, STATE, 128), F32),
                        pltpu.VMEM((CHUNK, D_INNER), F32)],
        compiler_params=_params("arbitrary"),
        name="ssd",
    )(xbc, dtr, zs, convw, convb, dtb, alog, dexp, normw, y_sample, xs_sample)


def _sgu_kernel(uvg_ref, lng_ref, lnb_ref, w_ref, bfull_ref, w0_ref, b0_ref, yb_ref, v_ref, vs_ref):
    i = pl.program_id(0)
    vn = _layer_norm(uvg_ref[:, D_GATE:2 * D_GATE], lng_ref[...], lnb_ref[...])

    @pl.when(i == BATCH * N_CHUNKS)
    def _():
        vs_ref[...] = vn
        s = w0_ref[...] * vn + b0_ref[...]
        yb_ref[...] = (uvg_ref[:, 0:D_GATE] * s).astype(yb_ref.dtype)

    @pl.when(i < BATCH * N_CHUNKS)
    def _():
        v_ref[0] = vn
        row = lax.broadcasted_iota(jnp.int32, (CHUNK, CHUNK), 0)
        col = lax.broadcasted_iota(jnp.int32, (CHUNK, CHUNK), 1)
        causal = row >= col
        for g in range(MLP_GROUPS):
            sl = slice(g * 128, (g + 1) * 128)
            wg = jnp.where(causal, w_ref[g], 0.0).astype(BF16)
            s = _dot(wg, vn[:, sl].astype(BF16)) + bfull_ref[:, sl]
            yb_ref[:, sl] = (uvg_ref[:, sl] * s).astype(yb_ref.dtype)


def _sgu(uvg, lng, lnb, w, bfull, w0, b0):
    const2 = lambda i: (0, 0)
    seq = lambda i: jnp.minimum(i // N_CHUNKS, BATCH - 1)
    return pl.pallas_call(
        _sgu_kernel,
        grid=(BATCH * N_CHUNKS + 1,),
        in_specs=[pl.BlockSpec((CHUNK, 2 * D_GATE), lambda i: (i, 0)),
                  pl.BlockSpec((1, D_GATE), const2),
                  pl.BlockSpec((1, D_GATE), const2),
                  pl.BlockSpec((MLP_GROUPS, CHUNK, CHUNK), lambda i: (0, 0, 0)),
                  pl.BlockSpec((CHUNK, D_GATE), const2),
                  pl.BlockSpec((1, D_GATE), const2),
                  pl.BlockSpec((1, D_GATE), const2)],
        out_specs=[pl.BlockSpec((CHUNK, D_GATE), lambda i: (i, 0)),
                   pl.BlockSpec((1, CHUNK, D_GATE), lambda i: (seq(i), 0, 0)),
                   pl.BlockSpec((DEC_BATCH, D_GATE), const2)],
        out_shape=[jax.ShapeDtypeStruct((T_ALL, D_GATE), BF16),
                   jax.ShapeDtypeStruct((BATCH, CHUNK, D_GATE), F32),
                   jax.ShapeDtypeStruct((DEC_BATCH, D_GATE), F32)],
        compiler_params=_params("arbitrary"),
        name="sgu",
    )(uvg, lng, lnb, w, bfull, w0, b0)


def _sample_pre_kernel(xbc_ref, sc_ref, dtr_ref, convw_ref, convb_ref, dtb_ref, alog_ref,
                       convnew_ref, xs_ref, xdt_ref, bm_ref, cm_ref, dec_ref):
    x = xbc_ref[...]
    acc = convb_ref[...] + convw_ref[3:4, :] * x
    acc = acc + convw_ref[2:3, :] * sc_ref[2]
    acc = acc + convw_ref[1:2, :] * sc_ref[1]
    acc = acc + convw_ref[0:1, :] * sc_ref[0]
    xc = _silu(acc)
    convnew_ref[0] = sc_ref[1]
    convnew_ref[1] = sc_ref[2]
    convnew_ref[2] = x
    dtv = _softplus(dtr_ref[...] + dtb_ref[...])
    dec_ref[...] = jnp.exp(dtv * (-jnp.exp(alog_ref[...])))
    hrow = lax.broadcasted_iota(jnp.int32, (DT_PAD, D_INNER), 0)
    col = lax.broadcasted_iota(jnp.int32, (DT_PAD, D_INNER), 1)
    rexp = jnp.where(lax.shift_right_logical(col, 6) == hrow, 1.0, 0.0).astype(F32)
    xs = xc[:, 0:D_INNER]
    xs_ref[...] = xs
    xdt_ref[...] = xs * _dot_f32(dtv, rexp)
    bm_ref[...] = xc[:, B_OFF:C_OFF]
    cm_ref[...] = xc[:, C_OFF:CONV_DIM]


def _sample_pre(xbc, sc_t, dtr, convw, convb, dtb, alog):
    blk = lambda i: (SAMPLE_BLOCK, 0)
    const = lambda i: (0, 0)
    return pl.pallas_call(
        _sample_pre_kernel,
        grid=(1,),
        in_specs=[pl.BlockSpec((DEC_BATCH, CONV_DIM), blk),
                  pl.BlockSpec((CONV_K - 1, DEC_BATCH, CONV_DIM), lambda i: (0, 0, 0)),
                  pl.BlockSpec((DEC_BATCH, DT_PAD), blk),
                  pl.BlockSpec((CONV_K, CONV_DIM), const),
                  pl.BlockSpec((1, CONV_DIM), const),
                  pl.BlockSpec((1, DT_PAD), const),
                  pl.BlockSpec((1, DT_PAD), const)],
        out_specs=[pl.BlockSpec((CONV_K - 1, DEC_BATCH, CONV_DIM), lambda i: (0, 0, 0)),
                   pl.BlockSpec((DEC_BATCH, D_INNER), const),
                   pl.BlockSpec((DEC_BATCH, D_INNER), const),
                   pl.BlockSpec((DEC_BATCH, GROUPS * STATE), const),
                   pl.BlockSpec((DEC_BATCH, GROUPS * STATE), const),
                   pl.BlockSpec((DEC_BATCH, DT_PAD), const)],
        out_shape=[jax.ShapeDtypeStruct((CONV_K - 1, DEC_BATCH, CONV_DIM), F32),
                   jax.ShapeDtypeStruct((DEC_BATCH, D_INNER), F32),
                   jax.ShapeDtypeStruct((DEC_BATCH, D_INNER), F32),
                   jax.ShapeDtypeStruct((DEC_BATCH, GROUPS * STATE), F32),
                   jax.ShapeDtypeStruct((DEC_BATCH, GROUPS * STATE), F32),
                   jax.ShapeDtypeStruct((DEC_BATCH, DT_PAD), F32)],
        compiler_params=_params("arbitrary"),
        name="sample_pre",
    )(xbc, sc_t, dtr, convw, convb, dtb, alog)


def _sample_state_kernel(dec_ref, xdtt_ref, bm_ref, cmt_ref, st_ref, new_ref, yt_ref):
    srow = lax.broadcasted_iota(jnp.int32, (DEC_BATCH, STATE), 0)
    slane = lax.broadcasted_iota(jnp.int32, (STATE, DEC_BATCH), 1)
    gw = HEADS_PER_GROUP * HEADDIM

    @pl.when(pl.program_id(0) == 0)
    def _():
        yt_ref[...] = jnp.zeros(yt_ref.shape, F32)

    for i in range(SAMPLES_PER_STEP):
        s = pl.program_id(0) * SAMPLES_PER_STEP + i
        brow = bm_ref[pl.ds(s, 1), :]
        for g in range(GROUPS):
            sel_b = jnp.where(srow == s, brow[:, g * STATE:(g + 1) * STATE], 0.0).astype(BF16)
            upd = _dot(xdtt_ref[g * gw:(g + 1) * gw, :], sel_b)
            news = []
            for r in range(HEADS_PER_GROUP):
                h = g * HEADS_PER_GROUP + r
                rows = slice(h * HEADDIM, (h + 1) * HEADDIM)
                new = st_ref[i, rows, :] * dec_ref[s, h] + upd[r * HEADDIM:(r + 1) * HEADDIM, :]
                new_ref[i, rows, :] = new
                news.append(new.astype(BF16))
            sel_c = jnp.where(slane == s, cmt_ref[g * STATE:(g + 1) * STATE, :], 0.0).astype(BF16)
            yt_ref[g * gw:(g + 1) * gw, :] += _dot(jnp.concatenate(news, axis=0), sel_c)


def _sample_state(dec, xdtt, bm, cmt, state):
    const = lambda s: (0, 0)
    blk = pl.BlockSpec((SAMPLES_PER_STEP, D_INNER, STATE), lambda s: (s, 0, 0))
    return pl.pallas_call(
        _sample_state_kernel,
        grid=(DEC_BATCH // SAMPLES_PER_STEP,),
        in_specs=[pl.BlockSpec(memory_space=pltpu.SMEM),
                  pl.BlockSpec((D_INNER, DEC_BATCH), const),
                  pl.BlockSpec((DEC_BATCH, GROUPS * STATE), const),
                  pl.BlockSpec((GROUPS * STATE, DEC_BATCH), const),
                  blk],
        out_specs=[blk, pl.BlockSpec((D_INNER, DEC_BATCH), const)],
        out_shape=[jax.ShapeDtypeStruct((DEC_BATCH, D_INNER, STATE), F32),
                   jax.ShapeDtypeStruct((D_INNER, DEC_BATCH), F32)],
        compiler_params=_params("arbitrary"),
        name="sample_state",
    )(dec, xdtt, bm, cmt, state)


def _merge_kernel(ya_ref, wa_ref, yb_ref, wb_ref, ga_ref, gb_ref, o_ref):
    o_ref[...] = (ga_ref[...] * _dot(ya_ref[...], wa_ref[...])
                  + gb_ref[...] * _dot(yb_ref[...], wb_ref[...])).astype(o_ref.dtype)


def _merge(ya, wa, yb, wb, gates):
    tm, tn = MM_TM, MM_TN // 2
    nb = D_MODEL // tn
    return pl.pallas_call(
        _merge_kernel,
        grid=(T_ALL // tm, nb),
        in_specs=[pl.BlockSpec((tm, D_INNER), lambda i, j: (i, 0)),
                  pl.BlockSpec((D_INNER, tn), lambda i, j: (0, j)),
                  pl.BlockSpec((tm, D_GATE), lambda i, j: (i, 0)),
                  pl.BlockSpec((D_GATE, tn), lambda i, j: (0, j)),
                  pl.BlockSpec((tm, tn), lambda i, j: (i, j)),
                  pl.BlockSpec((tm, tn), lambda i, j: (i, j + nb))],
        out_specs=pl.BlockSpec((tm, tn), lambda i, j: (i, j)),
        out_shape=jax.ShapeDtypeStruct((T_ALL, D_MODEL), BF16),
        compiler_params=_params("parallel", "parallel"),
        name="merge",
    )(ya, wa, yb, wb, gates, gates)


def _out_ln_kernel(m_ref, w_ref, x_ref, g_ref, b_ref, h_ref, hb_ref):
    h = _layer_norm(DN_ALPHA * x_ref[...] + _dot(m_ref[...], w_ref[...]), g_ref[...], b_ref[...])
    h_ref[...] = h
    hb_ref[...] = h.astype(BF16)


def _out_ln(merged, wout, x, g, b):
    tm = OUT_LN_TM
    const = lambda i: (0, 0)
    rows = lambda i: (i, 0)
    return pl.pallas_call(
        _out_ln_kernel,
        grid=(T_ALL // tm,),
        in_specs=[pl.BlockSpec((tm, D_MODEL), rows),
                  pl.BlockSpec((D_MODEL, D_MODEL), const),
                  pl.BlockSpec((tm, D_MODEL), rows),
                  pl.BlockSpec((1, D_MODEL), const),
                  pl.BlockSpec((1, D_MODEL), const)],
        out_specs=[pl.BlockSpec((tm, D_MODEL), rows), pl.BlockSpec((tm, D_MODEL), rows)],
        out_shape=[jax.ShapeDtypeStruct((T_ALL, D_MODEL), F32),
                   jax.ShapeDtypeStruct((T_ALL, D_MODEL), BF16)],
        compiler_params=_params("parallel"),
        name="out_ln",
    )(merged, wout, x, g, b)


def _final_ln_kernel(h_ref, pt_ref, g_ref, b_ref, o_ref):
    o_ref[...] = _layer_norm(DN_ALPHA * h_ref[...] + pt_ref[...].T, g_ref[...], b_ref[...])


def _final_ln(h, pt, g, b, *, tm, first_block, n_blocks, name):
    const = lambda i: (0, 0)
    rows = lambda i: (first_block + i, 0)
    return pl.pallas_call(
        _final_ln_kernel,
        grid=(n_blocks,),
        in_specs=[pl.BlockSpec((tm, D_MODEL), rows),
                  pl.BlockSpec((D_MODEL, tm), lambda i: (0, first_block + i)),
                  pl.BlockSpec((1, D_MODEL), const), pl.BlockSpec((1, D_MODEL), const)],
        out_specs=pl.BlockSpec((tm, D_MODEL), lambda i: (i, 0)),
        out_shape=jax.ShapeDtypeStruct((n_blocks * tm, D_MODEL), F32),
        compiler_params=_params("parallel"),
        name=name,
    )(h, pt, g, b)


def _compare_exchange(v, i, l, descending):
    hi, lo = jnp.maximum(v[i], v[l]), jnp.minimum(v[i], v[l])
    v[i], v[l] = (hi, lo) if descending else (lo, hi)


def _bitonic_merge_desc(v):
    v = list(v)
    j = len(v) // 2
    while j >= 1:
        for i in range(len(v)):
            if i ^ j > i:
                _compare_exchange(v, i, i ^ j, True)
        j //= 2
    return v


def _bitonic_sort_desc(v):
    v = list(v)
    k = 2
    while k <= len(v):
        j = k // 2
        while j >= 1:
            for i in range(len(v)):
                if i ^ j > i:
                    _compare_exchange(v, i, i ^ j, (i & k) == 0)
            j //= 2
        k *= 2
    return v


def _merge_sublanes_top16(v):
    for shift in (4, 6, 7):
        rolled = [pltpu.roll(a, shift, axis=0) for a in v]
        v = _bitonic_merge_desc([jnp.maximum(v[i], rolled[15 - i]) for i in range(16)])
    return [a[0:1, :] for a in v]


def _top16_of_rows(x):
    return _merge_sublanes_top16(_bitonic_sort_desc([x[8 * i:8 * i + 8, :] for i in range(16)]))


def _top16_of_list(rows):
    lanes = rows[0].shape[1]
    rows = rows + [jnp.full((1, lanes), NEG_INF, F32)] * (64 - len(rows))
    v = _bitonic_sort_desc([jnp.concatenate(rows[8 * i:8 * i + 8], axis=0) for i in range(8)])
    rolled = [pltpu.roll(a, 4, axis=0) for a in v]
    v = _bitonic_merge_desc(v + rolled[::-1])
    for shift in (6, 7):
        rolled = [pltpu.roll(a, shift, axis=0) for a in v]
        v = _bitonic_merge_desc([jnp.maximum(v[i], rolled[15 - i]) for i in range(16)])
    return [a[0:1, :] for a in v]


def _peer_front_kernel(qt_ref, keys_ref, s1_ref, e1_ref, s2_ref, e2_ref, thr_ref):
    k = PEER_TOPK
    for h in range(PEER_HEADS):
        sc = []
        top = []
        for side in range(2):
            hs = 2 * h + side
            x = _dot(keys_ref[hs], qt_ref[hs * PEER_KEYS:(hs + 1) * PEER_KEYS, :])
            sc.append(x)
            top.append(_top16_of_rows(x))
        best = _top16_of_list([top[0][i] + top[1][j] for i in range(k) for j in range(k)
                               if (i + 1) * (j + 1) <= k])
        z = jnp.zeros_like(best[0])
        for bk in best:
            z = z + jnp.exp(bk - best[0])
        s1_ref[h] = sc[0]
        s2_ref[h] = sc[1]
        e1_ref[h] = jnp.exp(sc[0] - top[0][0]) * (1.0 / z)
        e2_ref[h] = jnp.exp(sc[1] - top[1][0])
        thr_ref[h:h + 1, :] = best[k - 1]


def _peer_front(qt, keys):
    tt = 128
    blk3 = pl.BlockSpec((PEER_HEADS, PEER_KEYS, tt), lambda i: (0, 0, i))
    shp3 = jax.ShapeDtypeStruct((PEER_HEADS, PEER_KEYS, T_PAD), F32)
    return pl.pallas_call(
        _peer_front_kernel,
        grid=(T_PAD // tt,),
        in_specs=[pl.BlockSpec((2 * PEER_HEADS * PEER_KEYS, tt), lambda i: (0, i)),
                  pl.BlockSpec((2 * PEER_HEADS, PEER_KEYS, PEER_KEYS), lambda i: (0, 0, 0))],
        out_specs=[blk3, blk3, blk3, blk3, pl.BlockSpec((PEER_HEADS, tt), lambda i: (0, i))],
        out_shape=[shp3, shp3, shp3, shp3, jax.ShapeDtypeStruct((PEER_HEADS, T_PAD), F32)],
        compiler_params=_params("parallel"),
        name="peer_front",
    )(qt, keys)


def _peer_main_kernel(ht_ref, u_ref, vt_ref, s1_ref, e1_ref, s2_ref, e2_ref, thr_ref, o_ref, *act_s):
    j = pl.program_id(1)
    tt = ht_ref.shape[1]
    n_i1 = PEER_SUB // PEER_KEYS
    hk = PEER_GATE_ROWS
    n_chunks = tt // PEER_CHUNK
    chains = [(sub, c) for sub in range(PEER_EB // PEER_SUB) for c in range(n_chunks)]

    @pl.when(j == 0)
    def _():
        o_ref[...] = jnp.zeros(o_ref.shape, F32)

    slot = lax.rem(j, 2)

    def activations(k):
        sub, c = chains[k]
        act_s[k % 2][slot] = _dot(u_ref[sub * PEER_SUB:(sub + 1) * PEER_SUB, :],
                                  ht_ref[:, c * PEER_CHUNK:(c + 1) * PEER_CHUNK])
        return act_s[k % 2].at[slot]

    def gating(k, act, i1_list):
        sub, c = chains[k]
        tok_tiles = []
        for tc in range(PEER_CHUNK // 128):
            cols = slice(c * PEER_CHUNK + tc * 128, c * PEER_CHUNK + (tc + 1) * 128)
            lcols = slice(tc * 128, (tc + 1) * 128)
            parts = []
            for part in range(PEER_KEYS // hk):
                krows = slice(part * hk, (part + 1) * hk)
                gates = {r: jnp.zeros((hk, 128), F32) for r in i1_list}
                for h in range(PEER_HEADS):
                    s2t = s2_ref[h, krows, cols]
                    e2t = e2_ref[h, krows, cols]
                    thr = thr_ref[h:h + 1, cols]
                    s1t = s1_ref[h, :, cols]
                    e1t = e1_ref[h, :, cols]
                    for r in i1_list:
                        q = sub * n_i1 + r
                        pair = s1t[q:q + 1, :] + s2t
                        w = e1t[q:q + 1, :] * e2t
                        gates[r] = gates[r] + jnp.where(pair >= thr, w, 0.0)
                for r in i1_list:
                    arows = slice(r * PEER_KEYS + part * hk, r * PEER_KEYS + (part + 1) * hk)
                    gates[r] = gates[r] * _gelu(act[arows, lcols])
                parts.append(gates)
            tiles = [jnp.concatenate([p[r] for p in parts], axis=0).astype(BF16) for r in i1_list]
            tok_tiles.append(jnp.concatenate(tiles, axis=0))
        return jnp.concatenate(tok_tiles, axis=1)

    def accumulate(k, gt):
        sub, c = chains[k]
        o_ref[:, c * PEER_CHUNK:(c + 1) * PEER_CHUNK] += _dot(
            vt_ref[:, sub * PEER_SUB:(sub + 1) * PEER_SUB], gt)

    act = activations(0)
    for k in range(len(chains)):
        act_next = activations(k + 1) if k + 1 < len(chains) else None
        accumulate(k, gating(k, act, tuple(range(n_i1))))
        act = act_next


def _peer_main(ht, u, vt, s1, e1, s2, e2, thr):
    tt, eb = PEER_TT, PEER_EB
    once = pl.Buffered(1)
    blk3 = pl.BlockSpec((PEER_HEADS, PEER_KEYS, tt), lambda i, j: (0, 0, i), pipeline_mode=once)
    blk1 = pl.BlockSpec((PEER_HEADS, eb // PEER_KEYS, tt), lambda i, j: (0, j, i))
    return pl.pallas_call(
        _peer_main_kernel,
        grid=(T_PAD // tt, PEER_EXPERTS // eb),
        in_specs=[pl.BlockSpec((D_MODEL, tt), lambda i, j: (0, i), pipeline_mode=once),
                  pl.BlockSpec((eb, D_MODEL), lambda i, j: (j, 0)),
                  pl.BlockSpec((D_MODEL, eb), lambda i, j: (0, j)),
                  blk1, blk1, blk3, blk3,
                  pl.BlockSpec((PEER_HEADS, tt), lambda i, j: (0, i), pipeline_mode=once)],
        out_specs=pl.BlockSpec((D_MODEL, tt), lambda i, j: (0, i), pipeline_mode=once),
        out_shape=jax.ShapeDtypeStruct((D_MODEL, T_PAD), F32),
        scratch_shapes=[pltpu.VMEM((2, PEER_SUB, PEER_CHUNK), F32)] * 2,
        compiler_params=_params("parallel", "arbitrary"),
        name="peer_main",
    )(ht, u, vt, s1, e1, s2, e2, thr)


def kernel(x_prompt, x_sample, state_conv, state_ssm, w_in, conv_w, conv_b, dt_bias, a_log, d_skip,
           ssd_norm_w, sgu_ln_g, sgu_ln_b, sgu_w, sgu_b, w_branch_a, w_branch_b, w_out, ln1_g, ln1_b,
           peer_wq, peer_keys, peer_u, peer_v, ln2_g, ln2_b):
    row = lambda p: p[0].reshape(1, -1)
    x = jnp.concatenate([x_prompt.reshape(T_PROMPT, D_MODEL), x_sample.reshape(DEC_BATCH, D_MODEL)], axis=0)
    xb = x.astype(BF16)

    w = w_in[0]
    o1, o2, o3, o4 = D_INNER, D_INNER + CONV_DIM, D_INNER + CONV_DIM + HEADS, D_INNER + CONV_DIM + HEADS + 2 * D_GATE
    w_dt = jnp.pad(w[:, o2:o3], ((0, 0), (0, DT_PAD - HEADS)))
    mm = functools.partial(_matmul, tm=MM_TM)
    zs = mm(xb, w[:, :o1].astype(BF16), tn=MM_TN, act=_silu, name="proj_z")
    xbc = mm(xb, w[:, o1:o2].astype(BF16), tn=MM_TN, name="proj_xbc")
    dtr = mm(xb, w_dt.astype(BF16), tn=DT_PAD, name="proj_dt")
    uvg = mm(xb, w[:, o3:o4].astype(BF16), tn=MM_TN, act=_gelu, name="proj_uv")
    gates = mm(xb, w[:, o4:].astype(BF16), tn=MM_TN, act=jax.nn.sigmoid, name="proj_gates")

    pad_h = lambda p: jnp.pad(p[0], (0, DT_PAD - HEADS)).reshape(1, DT_PAD)
    dtb, alog = pad_h(dt_bias), pad_h(a_log)
    dexp = jnp.repeat(d_skip[0], HEADDIM).reshape(1, D_INNER)
    convw, convb, normw = conv_w[0], row(conv_b), row(ssd_norm_w)
    lng, lnb = row(sgu_ln_g), row(sgu_ln_b)

    sc_t = jnp.transpose(state_conv[0], (1, 0, 2))
    conv_s_t, xs_s, xdt_s, bm_s, cm_s, dec_s = _sample_pre(xbc, sc_t, dtr, convw, convb, dtb, alog)
    ssm_s, yt_s = _sample_state(dec_s, xdt_s.T.astype(BF16), bm_s, cm_s.T,
                                state_ssm[0].reshape(DEC_BATCH, D_INNER, STATE))

    ya, conv_p, ssm_p = _ssd(xbc, dtr, zs, convw, convb, dtb, alog, dexp, normw, yt_s.T, xs_s)
    bfull = jnp.repeat(sgu_b[0].T, CHUNK, axis=1)
    w0 = jnp.repeat(sgu_w[0][:, 0, 0], CHUNK).reshape(1, D_GATE)
    b0 = jnp.repeat(sgu_b[0][:, 0], CHUNK).reshape(1, D_GATE)
    yb, v_p, v_s = _sgu(uvg, lng, lnb, sgu_w[0], bfull, w0, b0)

    merged = _merge(ya, w_branch_a[0].astype(BF16), yb, w_branch_b[0].astype(BF16), gates)
    h, hb = _out_ln(merged, w_out[0].astype(BF16), x, row(ln1_g), row(ln1_b))

    ht = jnp.pad(hb.T, ((0, 0), (0, T_PAD - T_ALL)))
    qt = _matmul(peer_wq[0].T.astype(BF16), ht, tm=1024, tn=PEER_TT, out_dtype=BF16, name="peer_query")
    keys = peer_keys[0].reshape(2 * PEER_HEADS, PEER_KEYS, PEER_KEYS).astype(BF16)
    s1, e1, s2, e2, thr = _peer_front(qt, keys)
    p = _peer_main(ht, peer_u[0].astype(BF16), peer_v[0].T.astype(BF16), s1, e1, s2, e2, thr)
    y_prompt = _final_ln(h, p, row(ln2_g), row(ln2_b), tm=512, first_block=0, n_blocks=T_PROMPT // 512,
                         name="final_ln_prompt").reshape(BATCH, SEQ, D_MODEL)
    y_sample = _final_ln(h, p, row(ln2_g), row(ln2_b), tm=DEC_BATCH, first_block=SAMPLE_BLOCK, n_blocks=1,
                         name="final_ln_sample").reshape(DEC_BATCH, 1, D_MODEL)
    conv_sample = jnp.transpose(conv_s_t, (1, 0, 2))[None]
    return (y_prompt, y_sample, conv_p[None], ssm_p[None], v_p[None],
            conv_sample, ssm_s.reshape(1, DEC_BATCH, HEADS, HEADDIM, STATE), v_s[None, :, None, :])
```

```python
import functools
import math

import jax
import jax.numpy as jnp
from jax import lax
from jax.experimental import pallas as pl
from jax.experimental.pallas import tpu as pltpu

F32 = jnp.float32
BF16 = jnp.bfloat16

D_MODEL = 2048
SEQ = 2048
BATCH = 4
DEC_BATCH = 128
T_PROMPT = BATCH * SEQ
T_ALL = T_PROMPT + DEC_BATCH
CHUNK = 128
N_CHUNKS = SEQ // CHUNK
SAMPLE_BLOCK = T_PROMPT // CHUNK

D_INNER = 4096
HEADDIM = 64
HEADS = 64
GROUPS = 8
HEADS_PER_GROUP = HEADS // GROUPS
STATE = 128
CONV_K = 4
CONV_DIM = D_INNER + 2 * GROUPS * STATE
B_OFF = D_INNER
C_OFF = D_INNER + GROUPS * STATE
HEAD_PAIRS = HEADS // 2
DT_PAD = 128
D_GATE = 2048
MLP_GROUPS = 16
PEER_HEADS = 8
PEER_KEYS = 128
PEER_EXPERTS = PEER_KEYS * PEER_KEYS
PEER_TOPK = 16
DN_ALPHA = 2.0 ** 0.25
LN_EPS = 1e-5
RMS_EPS = 1e-5
NEG_INF = float("-inf")

OUT_LN_TM = 640
MM_TM = 2080
MERGE_TM = 1040
MM_TN = 1024
SAMPLES_PER_STEP = 4
PEER_EB = 2048
PEER_SUB = 512
PEER_CHUNK = 256
PEER_GATE_ROWS = 64
PEER_TT = 768
T_PAD = 8448
VMEM_LIMIT = 56 * 1024 * 1024


def _params(*sem):
    return pltpu.CompilerParams(dimension_semantics=sem, vmem_limit_bytes=VMEM_LIMIT)


def _gelu(x):
    return 0.5 * x * (1.0 + lax.erf(x * (1.0 / math.sqrt(2.0))))


def _silu(x):
    return x * jax.nn.sigmoid(x)


def _identity(x):
    return x


def _softplus(x):
    return jnp.maximum(x, 0.0) + jnp.log1p(jnp.exp(-jnp.abs(x)))


def _layer_norm(x, g, b):
    mu = jnp.mean(x, axis=-1, keepdims=True)
    xc = x - mu
    var = jnp.mean(xc * xc, axis=-1, keepdims=True)
    return xc * lax.rsqrt(var + LN_EPS) * g + b


def _dot(a, b):
    return jnp.dot(a, b, preferred_element_type=F32)


def _dot_f32(a, b):
    return jnp.dot(a, b, preferred_element_type=F32, precision=lax.Precision.HIGHEST)


def _mm_kernel(a_ref, b_ref, o_ref, *, act):
    o_ref[...] = act(_dot(a_ref[...], b_ref[...])).astype(o_ref.dtype)


def _matmul(a, b, *, tm, tn, name, act=_identity, out_dtype=F32):
    m, k = a.shape
    n = b.shape[1]
    return pl.pallas_call(
        functools.partial(_mm_kernel, act=act),
        grid=(m // tm, n // tn),
        in_specs=[pl.BlockSpec((tm, k), lambda i, j: (i, 0)),
                  pl.BlockSpec((k, tn), lambda i, j: (0, j))],
        out_specs=pl.BlockSpec((tm, tn), lambda i, j: (i, j)),
        out_shape=jax.ShapeDtypeStruct((m, n), out_dtype),
        compiler_params=_params("parallel", "parallel"),
        name=name,
    )(a, b)


def _ssd_gate_norm(y, xs, zs, dexp, normw):
    yg = (y + dexp * xs) * zs
    gw = D_INNER // GROUPS
    outs = []
    for g in range(GROUPS):
        blk = yg[:, g * gw:(g + 1) * gw]
        ms = jnp.mean(blk * blk, axis=-1, keepdims=True)
        outs.append(blk * lax.rsqrt(ms + RMS_EPS) * normw[:, g * gw:(g + 1) * gw])
    return outs


def _dot_exact01(x, r01):
    hi = x.astype(BF16)
    rest = x - hi.astype(F32)
    mid = rest.astype(BF16)
    lo = (rest - mid.astype(F32)).astype(BF16)
    return (_dot(hi, r01) + _dot(mid, r01)) + _dot(lo, r01)


def _ssd_kernel(xbc_ref, dtr_ref, zs_ref, convw_ref, convb_ref, dtb_ref, alog_ref,
                dexp_ref, normw_ref, ys_ref, xss_ref,
                y_ref, convst_ref, ssm_ref,
                xext, xc_s, rexp_s, rcol_s, colb_s, acumt_s, dtt_s, wt_s, stt_s, ysc_s):
    i = pl.program_id(0)
    gw = D_INNER // GROUPS

    @pl.when(i == 0)
    def _():
        hrow = lax.broadcasted_iota(jnp.int32, (DT_PAD, D_INNER), 0)
        col = lax.broadcasted_iota(jnp.int32, (DT_PAD, D_INNER), 1)
        rexp_s[...] = jnp.where(lax.shift_right_logical(col, 6) == hrow, 1.0, 0.0).astype(BF16)
        hrow = lax.broadcasted_iota(jnp.int32, (DT_PAD, HEADS * CHUNK), 0)
        col = lax.broadcasted_iota(jnp.int32, (DT_PAD, HEADS * CHUNK), 1)
        rcol_s[...] = jnp.where(lax.shift_right_logical(col, 7) == hrow, 1.0, 0.0).astype(BF16)

    @pl.when(i == BATCH * N_CHUNKS)
    def _():
        outs = _ssd_gate_norm(ys_ref[...], xss_ref[...], zs_ref[...], dexp_ref[...], normw_ref[...])
        for g in range(GROUPS):
            y_ref[:, g * gw:(g + 1) * gw] = outs[g].astype(y_ref.dtype)

    @pl.when(i < BATCH * N_CHUNKS)
    def _():
        _ssd_prompt_chunk(i % N_CHUNKS, xbc_ref, dtr_ref, zs_ref, convw_ref, convb_ref, dtb_ref, alog_ref,
                          dexp_ref, normw_ref, y_ref, convst_ref, ssm_ref,
                          xext, xc_s, rexp_s, rcol_s, colb_s, acumt_s, dtt_s, wt_s, stt_s, ysc_s)


def _ssd_prompt_chunk(c, xbc_ref, dtr_ref, zs_ref, convw_ref, convb_ref, dtb_ref, alog_ref,
                      dexp_ref, normw_ref, y_ref, convst_ref, ssm_ref,
                      xext, xc_s, rexp_s, rcol_s, colb_s, acumt_s, dtt_s, wt_s, stt_s, ysc_s):
    gw = D_INNER // GROUPS

    @pl.when(c == 0)
    def _():
        xext[0:8, :] = jnp.zeros((8, CONV_DIM), F32)
        stt_s[...] = jnp.zeros(stt_s.shape, F32)

    xext[8:8 + CHUNK, :] = xbc_ref[...]
    slab = 512
    for s0 in range(0, CONV_DIM, slab):
        sl = slice(s0, s0 + slab)
        ext = xext[:, sl]
        acc = convw_ref[0:1, sl] * ext
        acc = convw_ref[1:2, sl] * ext + pltpu.roll(acc, 1, axis=0)
        acc = convw_ref[2:3, sl] * ext + pltpu.roll(acc, 1, axis=0)
        acc = convw_ref[3:4, sl] * ext + pltpu.roll(acc, 1, axis=0)
        xc_s[:, sl] = _silu(convb_ref[:, sl] + acc[8:8 + CHUNK, :])
    convst_ref[0] = xbc_ref[CHUNK - 3:CHUNK, :]
    xext[0:8, :] = xbc_ref[CHUNK - 8:CHUNK, :]

    dtv = _softplus(dtr_ref[...] + dtb_ref[...])
    a = dtv * (-jnp.exp(alog_ref[...]))
    row = lax.broadcasted_iota(jnp.int32, (CHUNK, CHUNK), 0)
    colq = lax.broadcasted_iota(jnp.int32, (CHUNK, CHUNK), 1)
    causal = row >= colq
    acum = _dot_f32(jnp.where(causal, 1.0, 0.0).astype(F32), a)
    last = acum[CHUNK - 1:CHUNK, :]
    acumt_s[...] = acum.T
    dtt_s[...] = dtv.T
    wt_s[...] = (dtv * jnp.exp(last - acum)).T
    colb_s[...] = _dot_exact01(acum, rcol_s[...])
    chunk_decay = jnp.exp(_dot_exact01(jnp.broadcast_to(last, (8, DT_PAD)), rexp_s[...]))[0:1, :]

    lane = lax.broadcasted_iota(jnp.int32, (CHUNK, 128), 1)
    lo = lane < HEADDIM
    zero_b = jnp.zeros((CHUNK, 128), BF16)
    for g in range(GROUPS):
        bg = xc_s[:, B_OFF + g * STATE:B_OFF + (g + 1) * STATE]
        cg = xc_s[:, C_OFF + g * STATE:C_OFF + (g + 1) * STATE]
        cb = lax.dot_general(cg.astype(BF16), bg.astype(BF16), (((1,), (1,)), ((), ())),
                             preferred_element_type=F32)
        bgt = bg.T
        for pr in range(HEADS_PER_GROUP // 2):
            hp = g * (HEADS_PER_GROUP // 2) + pr
            psl = slice(hp * 128, (hp + 1) * 128)
            heads = (2 * hp, 2 * hp + 1)
            lhs = []
            for hh in heads:
                colb = colb_s[:, hh * CHUNK:(hh + 1) * CHUNK]
                seg = colb - acumt_s[hh:hh + 1, :]
                decay = jnp.exp(jnp.where(causal, seg, NEG_INF))
                lhs.append((cb * decay * dtt_s[hh:hh + 1, :]).astype(BF16))
            for hh in heads:
                colb = colb_s[:, hh * CHUNK:(hh + 1) * CHUNK]
                lhs.append((cg * jnp.exp(colb)).astype(BF16))
            xpair = xc_s[:, psl].astype(BF16)
            spair = stt_s[hp].astype(BF16)
            x_lo, x_hi = jnp.where(lo, xpair, zero_b), jnp.where(lo, zero_b, xpair)
            rhs = jnp.concatenate([x_lo, x_hi, jnp.where(lo, spair, zero_b), jnp.where(lo, zero_b, spair)],
                                  axis=0)
            ysc_s[:, psl] = _dot(jnp.concatenate(lhs, axis=1), rhs)
            bw = jnp.concatenate([(bgt * wt_s[hh:hh + 1, :]).astype(BF16) for hh in heads], axis=1)
            upd = _dot(bw, jnp.concatenate([x_lo, x_hi], axis=0))
            stt_s[hp] = stt_s[hp] * chunk_decay[:, psl] + upd

    outs = _ssd_gate_norm(ysc_s[...], xc_s[:, 0:D_INNER], zs_ref[...], dexp_ref[...], normw_ref[...])
    for g in range(GROUPS):
        y_ref[:, g * gw:(g + 1) * gw] = outs[g].astype(y_ref.dtype)

    @pl.when(c == N_CHUNKS - 1)
    def _():
        for hp in range(HEAD_PAIRS):
            st = stt_s[hp].T
            ssm_ref[0, 2 * hp] = st[0:HEADDIM, :]
            ssm_ref[0, 2 * hp + 1] = st[HEADDIM:2 * HEADDIM, :]


def _ssd(xbc, dtr, zs, convw, convb, dtb, alog, dexp, normw, y_sample, xs_sample):
    rowblk = lambda i: (i, 0)
    const = lambda i: (0, 0)
    seq = lambda i: jnp.minimum(i // N_CHUNKS, BATCH - 1)
    return pl.pallas_call(
        _ssd_kernel,
        grid=(BATCH * N_CHUNKS + 1,),
        in_specs=[pl.BlockSpec((CHUNK, CONV_DIM), rowblk),
                  pl.BlockSpec((CHUNK, DT_PAD), rowblk),
                  pl.BlockSpec((CHUNK, D_INNER), rowblk),
                  pl.BlockSpec((CONV_K, CONV_DIM), const),
                  pl.BlockSpec((1, CONV_DIM), const),
                  pl.BlockSpec((1, DT_PAD), const),
                  pl.BlockSpec((1, DT_PAD), const),
                  pl.BlockSpec((1, D_INNER), const),
                  pl.BlockSpec((1, D_INNER), const),
                  pl.BlockSpec((DEC_BATCH, D_INNER), const),
                  pl.BlockSpec((DEC_BATCH, D_INNER), const)],
        out_specs=[pl.BlockSpec((CHUNK, D_INNER), rowblk),
                   pl.BlockSpec((1, CONV_K - 1, CONV_DIM), lambda i: (seq(i), 0, 0)),
                   pl.BlockSpec((1, HEADS, HEADDIM, STATE), lambda i: (seq(i), 0, 0, 0))],
        out_shape=[jax.ShapeDtypeStruct((T_ALL, D_INNER), BF16),
                   jax.ShapeDtypeStruct((BATCH, CONV_K - 1, CONV_DIM), F32),
                   jax.ShapeDtypeStruct((BATCH, HEADS, HEADDIM, STATE), F32)],
        scratch_shapes=[pltpu.VMEM((8 + CHUNK, CONV_DIM), F32),
                        pltpu.VMEM((CHUNK, CONV_DIM), F32),
                        pltpu.VMEM((DT_PAD, D_INNER), BF16),
                        pltpu.VMEM((DT_PAD, HEADS * CHUNK), BF16),
                        pltpu.VMEM((CHUNK, HEADS * CHUNK), F32),
                        pltpu.VMEM((DT_PAD, CHUNK), F32),
                        pltpu.VMEM((DT_PAD, CHUNK), F32),
                        pltpu.VMEM((DT_PAD, CHUNK), F32),
                        pltpu.VMEM((HEAD_PAIRS, STATE, 128), F32),
                        pltpu.VMEM((CHUNK, D_INNER), F32)],
        compiler_params=_params("arbitrary"),
        name="ssd",
    )(xbc, dtr, zs, convw, convb, dtb, alog, dexp, normw, y_sample, xs_sample)


def _sgu_kernel(uvg_ref, lng_ref, lnb_ref, w_ref, bfull_ref, w0_ref, b0_ref, yb_ref, v_ref, vs_ref):
    i = pl.program_id(0)
    vn = _layer_norm(uvg_ref[:, D_GATE:2 * D_GATE], lng_ref[...], lnb_ref[...])

    @pl.when(i == BATCH * N_CHUNKS)
    def _():
        vs_ref[...] = vn
        s = w0_ref[...] * vn + b0_ref[...]
        yb_ref[...] = (uvg_ref[:, 0:D_GATE] * s).astype(yb_ref.dtype)

    @pl.when(i < BATCH * N_CHUNKS)
    def _():
        v_ref[0] = vn
        row = lax.broadcasted_iota(jnp.int32, (CHUNK, CHUNK), 0)
        col = lax.broadcasted_iota(jnp.int32, (CHUNK, CHUNK), 1)
        causal = row >= col
        for g in range(MLP_GROUPS):
            sl = slice(g * 128, (g + 1) * 128)
            wg = jnp.where(causal, w_ref[g], 0.0).astype(BF16)
            s = _dot(wg, vn[:, sl].astype(BF16)) + bfull_ref[:, sl]
            yb_ref[:, sl] = (uvg_ref[:, sl] * s).astype(yb_ref.dtype)


def _sgu(uvg, lng, lnb, w, bfull, w0, b0):
    const2 = lambda i: (0, 0)
    seq = lambda i: jnp.minimum(i // N_CHUNKS, BATCH - 1)
    return pl.pallas_call(
        _sgu_kernel,
        grid=(BATCH * N_CHUNKS + 1,),
        in_specs=[pl.BlockSpec((CHUNK, 2 * D_GATE), lambda i: (i, 0)),
                  pl.BlockSpec((1, D_GATE), const2),
                  pl.BlockSpec((1, D_GATE), const2),
                  pl.BlockSpec((MLP_GROUPS, CHUNK, CHUNK), lambda i: (0, 0, 0)),
                  pl.BlockSpec((CHUNK, D_GATE), const2),
                  pl.BlockSpec((1, D_GATE), const2),
                  pl.BlockSpec((1, D_GATE), const2)],
        out_specs=[pl.BlockSpec((CHUNK, D_GATE), lambda i: (i, 0)),
                   pl.BlockSpec((1, CHUNK, D_GATE), lambda i: (seq(i), 0, 0)),
                   pl.BlockSpec((DEC_BATCH, D_GATE), const2)],
        out_shape=[jax.ShapeDtypeStruct((T_ALL, D_GATE), BF16),
                   jax.ShapeDtypeStruct((BATCH, CHUNK, D_GATE), F32),
                   jax.ShapeDtypeStruct((DEC_BATCH, D_GATE), F32)],
        compiler_params=_params("arbitrary"),
        name="sgu",
    )(uvg, lng, lnb, w, bfull, w0, b0)


def _sample_pre_kernel(xbc_ref, sc_ref, dtr_ref, convw_ref, convb_ref, dtb_ref, alog_ref,
                       convnew_ref, xs_ref, xdt_ref, bm_ref, cm_ref, dec_ref):
    x = xbc_ref[...]
    acc = convb_ref[...] + convw_ref[3:4, :] * x
    acc = acc + convw_ref[2:3, :] * sc_ref[2]
    acc = acc + convw_ref[1:2, :] * sc_ref[1]
    acc = acc + convw_ref[0:1, :] * sc_ref[0]
    xc = _silu(acc)
    convnew_ref[0] = sc_ref[1]
    convnew_ref[1] = sc_ref[2]
    convnew_ref[2] = x
    dtv = _softplus(dtr_ref[...] + dtb_ref[...])
    dec_ref[...] = jnp.exp(dtv * (-jnp.exp(alog_ref[...])))
    hrow = lax.broadcasted_iota(jnp.int32, (DT_PAD, D_INNER), 0)
    col = lax.broadcasted_iota(jnp.int32, (DT_PAD, D_INNER), 1)
    rexp = jnp.where(lax.shift_right_logical(col, 6) == hrow, 1.0, 0.0).astype(F32)
    xs = xc[:, 0:D_INNER]
    xs_ref[...] = xs
    xdt_ref[...] = xs * _dot_f32(dtv, rexp)
    bm_ref[...] = xc[:, B_OFF:C_OFF]
    cm_ref[...] = xc[:, C_OFF:CONV_DIM]


def _sample_pre(xbc, sc_t, dtr, convw, convb, dtb, alog):
    blk = lambda i: (SAMPLE_BLOCK, 0)
    const = lambda i: (0, 0)
    return pl.pallas_call(
        _sample_pre_kernel,
        grid=(1,),
        in_specs=[pl.BlockSpec((DEC_BATCH, CONV_DIM), blk),
                  pl.BlockSpec((CONV_K - 1, DEC_BATCH, CONV_DIM), lambda i: (0, 0, 0)),
                  pl.BlockSpec((DEC_BATCH, DT_PAD), blk),
                  pl.BlockSpec((CONV_K, CONV_DIM), const),
                  pl.BlockSpec((1, CONV_DIM), const),
                  pl.BlockSpec((1, DT_PAD), const),
                  pl.BlockSpec((1, DT_PAD), const)],
        out_specs=[pl.BlockSpec((CONV_K - 1, DEC_BATCH, CONV_DIM), lambda i: (0, 0, 0)),
                   pl.BlockSpec((DEC_BATCH, D_INNER), const),
                   pl.BlockSpec((DEC_BATCH, D_INNER), const),
                   pl.BlockSpec((DEC_BATCH, GROUPS * STATE), const),
                   pl.BlockSpec((DEC_BATCH, GROUPS * STATE), const),
                   pl.BlockSpec((DEC_BATCH, DT_PAD), const)],
        out_shape=[jax.ShapeDtypeStruct((CONV_K - 1, DEC_BATCH, CONV_DIM), F32),
                   jax.ShapeDtypeStruct((DEC_BATCH, D_INNER), F32),
                   jax.ShapeDtypeStruct((DEC_BATCH, D_INNER), F32),
                   jax.ShapeDtypeStruct((DEC_BATCH, GROUPS * STATE), F32),
                   jax.ShapeDtypeStruct((DEC_BATCH, GROUPS * STATE), F32),
                   jax.ShapeDtypeStruct((DEC_BATCH, DT_PAD), F32)],
        compiler_params=_params("arbitrary"),
        name="sample_pre",
    )(xbc, sc_t, dtr, convw, convb, dtb, alog)


def _sample_state_kernel(dec_ref, xdtt_ref, bm_ref, cmt_ref, st_ref, new_ref, yt_ref):
    srow = lax.broadcasted_iota(jnp.int32, (DEC_BATCH, STATE), 0)
    slane = lax.broadcasted_iota(jnp.int32, (STATE, DEC_BATCH), 1)
    gw = HEADS_PER_GROUP * HEADDIM

    @pl.when(pl.program_id(0) == 0)
    def _():
        yt_ref[...] = jnp.zeros(yt_ref.shape, F32)

    for i in range(SAMPLES_PER_STEP):
        s = pl.program_id(0) * SAMPLES_PER_STEP + i
        brow = bm_ref[pl.ds(s, 1), :]
        for g in range(GROUPS):
            sel_b = jnp.where(srow == s, brow[:, g * STATE:(g + 1) * STATE], 0.0).astype(BF16)
            upd = _dot(xdtt_ref[g * gw:(g + 1) * gw, :], sel_b)
            news = []
            for r in range(HEADS_PER_GROUP):
                h = g * HEADS_PER_GROUP + r
                rows = slice(h * HEADDIM, (h + 1) * HEADDIM)
                new = st_ref[i, rows, :] * dec_ref[s, h] + upd[r * HEADDIM:(r + 1) * HEADDIM, :]
                new_ref[i, rows, :] = new
                news.append(new.astype(BF16))
            sel_c = jnp.where(slane == s, cmt_ref[g * STATE:(g + 1) * STATE, :], 0.0).astype(BF16)
            yt_ref[g * gw:(g + 1) * gw, :] += _dot(jnp.concatenate(news, axis=0), sel_c)


def _sample_state(dec, xdtt, bm, cmt, state):
    const = lambda s: (0, 0)
    blk = pl.BlockSpec((SAMPLES_PER_STEP, D_INNER, STATE), lambda s: (s, 0, 0))
    return pl.pallas_call(
        _sample_state_kernel,
        grid=(DEC_BATCH // SAMPLES_PER_STEP,),
        in_specs=[pl.BlockSpec(memory_space=pltpu.SMEM),
                  pl.BlockSpec((D_INNER, DEC_BATCH), const),
                  pl.BlockSpec((DEC_BATCH, GROUPS * STATE), const),
                  pl.BlockSpec((GROUPS * STATE, DEC_BATCH), const),
                  blk],
        out_specs=[blk, pl.BlockSpec((D_INNER, DEC_BATCH), const)],
        out_shape=[jax.ShapeDtypeStruct((DEC_BATCH, D_INNER, STATE), F32),
                   jax.ShapeDtypeStruct((D_INNER, DEC_BATCH), F32)],
        compiler_params=_params("arbitrary"),
        name="sample_state",
    )(dec, xdtt, bm, cmt, state)


def _merge_kernel(ya_ref, wa_ref, yb_ref, wb_ref, ga_ref, gb_ref, o_ref):
    o_ref[...] = (ga_ref[...] * _dot(ya_ref[...], wa_ref[...])
                  + gb_ref[...] * _dot(yb_ref[...], wb_ref[...])).astype(o_ref.dtype)


def _merge(ya, wa, yb, wb, gates):
    tm, tn = MERGE_TM, MM_TN // 2
    nb = D_MODEL // tn
    return pl.pallas_call(
        _merge_kernel,
        grid=(T_ALL // tm, nb),
        in_specs=[pl.BlockSpec((tm, D_INNER), lambda i, j: (i, 0)),
                  pl.BlockSpec((D_INNER, tn), lambda i, j: (0, j)),
                  pl.BlockSpec((tm, D_GATE), lambda i, j: (i, 0)),
                  pl.BlockSpec((D_GATE, tn), lambda i, j: (0, j)),
                  pl.BlockSpec((tm, tn), lambda i, j: (i, j)),
                  pl.BlockSpec((tm, tn), lambda i, j: (i, j + nb))],
        out_specs=pl.BlockSpec((tm, tn), lambda i, j: (i, j)),
        out_shape=jax.ShapeDtypeStruct((T_ALL, D_MODEL), BF16),
        compiler_params=_params("parallel", "parallel"),
        name="merge",
    )(ya, wa, yb, wb, gates, gates)


def _out_ln_kernel(m_ref, w_ref, x_ref, g_ref, b_ref, h_ref, hb_ref):
    h = _layer_norm(DN_ALPHA * x_ref[...] + _dot(m_ref[...], w_ref[...]), g_ref[...], b_ref[...])
    h_ref[...] = h
    hb_ref[...] = h.astype(BF16)


def _out_ln(merged, wout, x, g, b):
    tm = OUT_LN_TM
    const = lambda i: (0, 0)
    rows = lambda i: (i, 0)
    return pl.pallas_call(
        _out_ln_kernel,
        grid=(T_ALL // tm,),
        in_specs=[pl.BlockSpec((tm, D_MODEL), rows),
                  pl.BlockSpec((D_MODEL, D_MODEL), const),
                  pl.BlockSpec((tm, D_MODEL), rows),
                  pl.BlockSpec((1, D_MODEL), const),
                  pl.BlockSpec((1, D_MODEL), const)],
        out_specs=[pl.BlockSpec((tm, D_MODEL), rows), pl.BlockSpec((tm, D_MODEL), rows)],
        out_shape=[jax.ShapeDtypeStruct((T_ALL, D_MODEL), F32),
                   jax.ShapeDtypeStruct((T_ALL, D_MODEL), BF16)],
        compiler_params=_params("parallel"),
        name="out_ln",
    )(merged, wout, x, g, b)


def _final_ln_kernel(h_ref, pt_ref, g_ref, b_ref, o_ref):
    o_ref[...] = _layer_norm(DN_ALPHA * h_ref[...] + pt_ref[...].T, g_ref[...], b_ref[...])


def _final_ln(h, pt, g, b, *, tm, first_block, n_blocks, name):
    const = lambda i: (0, 0)
    rows = lambda i: (first_block + i, 0)
    return pl.pallas_call(
        _final_ln_kernel,
        grid=(n_blocks,),
        in_specs=[pl.BlockSpec((tm, D_MODEL), rows),
                  pl.BlockSpec((D_MODEL, tm), lambda i: (0, first_block + i)),
                  pl.BlockSpec((1, D_MODEL), const), pl.BlockSpec((1, D_MODEL), const)],
        out_specs=pl.BlockSpec((tm, D_MODEL), lambda i: (i, 0)),
        out_shape=jax.ShapeDtypeStruct((n_blocks * tm, D_MODEL), F32),
        compiler_params=_params("parallel"),
        name=name,
    )(h, pt, g, b)


def _compare_exchange(v, i, l, descending):
    hi, lo = jnp.maximum(v[i], v[l]), jnp.minimum(v[i], v[l])
    v[i], v[l] = (hi, lo) if descending else (lo, hi)


def _bitonic_merge_desc(v):
    v = list(v)
    j = len(v) // 2
    while j >= 1:
        for i in range(len(v)):
            if i ^ j > i:
                _compare_exchange(v, i, i ^ j, True)
        j //= 2
    return v


def _bitonic_sort_desc(v):
    v = list(v)
    k = 2
    while k <= len(v):
        j = k // 2
        while j >= 1:
            for i in range(len(v)):
                if i ^ j > i:
                    _compare_exchange(v, i, i ^ j, (i & k) == 0)
            j //= 2
        k *= 2
    return v


def _merge_sublanes_top16(v):
    for shift in (4, 6, 7):
        rolled = [pltpu.roll(a, shift, axis=0) for a in v]
        v = _bitonic_merge_desc([jnp.maximum(v[i], rolled[15 - i]) for i in range(16)])
    return [a[0:1, :] for a in v]


def _top16_of_rows(x):
    return _merge_sublanes_top16(_bitonic_sort_desc([x[8 * i:8 * i + 8, :] for i in range(16)]))


def _top16_of_list(rows):
    lanes = rows[0].shape[1]
    rows = rows + [jnp.full((1, lanes), NEG_INF, F32)] * (64 - len(rows))
    v = _bitonic_sort_desc([jnp.concatenate(rows[8 * i:8 * i + 8], axis=0) for i in range(8)])
    rolled = [pltpu.roll(a, 4, axis=0) for a in v]
    v = _bitonic_merge_desc(v + rolled[::-1])
    for shift in (6, 7):
        rolled = [pltpu.roll(a, shift, axis=0) for a in v]
        v = _bitonic_merge_desc([jnp.maximum(v[i], rolled[15 - i]) for i in range(16)])
    return [a[0:1, :] for a in v]


def _peer_front_kernel(qt_ref, keys_ref, s1_ref, e1_ref, s2_ref, e2_ref, thr_ref):
    k = PEER_TOPK
    for h in range(PEER_HEADS):
        sc = []
        top = []
        for side in range(2):
            hs = 2 * h + side
            x = _dot(keys_ref[hs], qt_ref[hs * PEER_KEYS:(hs + 1) * PEER_KEYS, :])
            sc.append(x)
            top.append(_top16_of_rows(x))
        best = _top16_of_list([top[0][i] + top[1][j] for i in range(k) for j in range(k)
                               if (i + 1) * (j + 1) <= k])
        z = jnp.zeros_like(best[0])
        for bk in best:
            z = z + jnp.exp(bk - best[0])
        s1_ref[h] = sc[0]
        s2_ref[h] = sc[1]
        e1_ref[h] = jnp.exp(sc[0] - top[0][0]) * (1.0 / z)
        e2_ref[h] = jnp.exp(sc[1] - top[1][0])
        thr_ref[h:h + 1, :] = best[k - 1]


def _peer_front(qt, keys):
    tt = 128
    blk3 = pl.BlockSpec((PEER_HEADS, PEER_KEYS, tt), lambda i: (0, 0, i))
    shp3 = jax.ShapeDtypeStruct((PEER_HEADS, PEER_KEYS, T_PAD), F32)
    return pl.pallas_call(
        _peer_front_kernel,
        grid=(T_PAD // tt,),
        in_specs=[pl.BlockSpec((2 * PEER_HEADS * PEER_KEYS, tt), lambda i: (0, i)),
                  pl.BlockSpec((2 * PEER_HEADS, PEER_KEYS, PEER_KEYS), lambda i: (0, 0, 0))],
        out_specs=[blk3, blk3, blk3, blk3, pl.BlockSpec((PEER_HEADS, tt), lambda i: (0, i))],
        out_shape=[shp3, shp3, shp3, shp3, jax.ShapeDtypeStruct((PEER_HEADS, T_PAD), F32)],
        compiler_params=_params("parallel"),
        name="peer_front",
    )(qt, keys)


def _peer_main_kernel(ht_ref, u_ref, vt_ref, s1_ref, e1_ref, s2_ref, e2_ref, thr_ref, o_ref, *act_s):
    j = pl.program_id(1)
    tt = ht_ref.shape[1]
    n_i1 = PEER_SUB // PEER_KEYS
    hk = PEER_GATE_ROWS
    n_chunks = tt // PEER_CHUNK
    chains = [(sub, c) for sub in range(PEER_EB // PEER_SUB) for c in range(n_chunks)]

    @pl.when(j == 0)
    def _():
        o_ref[...] = jnp.zeros(o_ref.shape, F32)

    slot = lax.rem(j, 2)

    def activations(k):
        sub, c = chains[k]
        act_s[k % 2][slot] = _dot(u_ref[sub * PEER_SUB:(sub + 1) * PEER_SUB, :],
                                  ht_ref[:, c * PEER_CHUNK:(c + 1) * PEER_CHUNK])
        return act_s[k % 2].at[slot]

    def gating(k, act, i1_list):
        sub, c = chains[k]
        tok_tiles = []
        for tc in range(PEER_CHUNK // 128):
            cols = slice(c * PEER_CHUNK + tc * 128, c * PEER_CHUNK + (tc + 1) * 128)
            lcols = slice(tc * 128, (tc + 1) * 128)
            parts = []
            for part in range(PEER_KEYS // hk):
                krows = slice(part * hk, (part + 1) * hk)
                gates = {r: jnp.zeros((hk, 128), F32) for r in i1_list}
                for h in range(PEER_HEADS):
                    s2t = s2_ref[h, krows, cols]
                    e2t = e2_ref[h, krows, cols]
                    thr = thr_ref[h:h + 1, cols]
                    s1t = s1_ref[h, :, cols]
                    e1t = e1_ref[h, :, cols]
                    for r in i1_list:
                        q = sub * n_i1 + r
                        pair = s1t[q:q + 1, :] + s2t
                        w = e1t[q:q + 1, :] * e2t
                        gates[r] = gates[r] + jnp.where(pair >= thr, w, 0.0)
                for r in i1_list:
                    arows = slice(r * PEER_KEYS + part * hk, r * PEER_KEYS + (part + 1) * hk)
                    gates[r] = gates[r] * _gelu(act[arows, lcols])
                parts.append(gates)
            tiles = [jnp.concatenate([p[r] for p in parts], axis=0).astype(BF16) for r in i1_list]
            tok_tiles.append(jnp.concatenate(tiles, axis=0))
        return jnp.concatenate(tok_tiles, axis=1)

    def accumulate(k, gt):
        sub, c = chains[k]
        o_ref[:, c * PEER_CHUNK:(c + 1) * PEER_CHUNK] += _dot(
            vt_ref[:, sub * PEER_SUB:(sub + 1) * PEER_SUB], gt)

    act = activations(0)
    for k in range(len(chains)):
        act_next = activations(k + 1) if k + 1 < len(chains) else None
        accumulate(k, gating(k, act, tuple(range(n_i1))))
        act = act_next


def _peer_main(ht, u, vt, s1, e1, s2, e2, thr):
    tt, eb = PEER_TT, PEER_EB
    once = pl.Buffered(1)
    blk3 = pl.BlockSpec((PEER_HEADS, PEER_KEYS, tt), lambda i, j: (0, 0, i), pipeline_mode=once)
    blk1 = pl.BlockSpec((PEER_HEADS, eb // PEER_KEYS, tt), lambda i, j: (0, j, i))
    return pl.pallas_call(
        _peer_main_kernel,
        grid=(T_PAD // tt, PEER_EXPERTS // eb),
        in_specs=[pl.BlockSpec((D_MODEL, tt), lambda i, j: (0, i), pipeline_mode=once),
                  pl.BlockSpec((eb, D_MODEL), lambda i, j: (j, 0)),
                  pl.BlockSpec((D_MODEL, eb), lambda i, j: (0, j)),
                  blk1, blk1, blk3, blk3,
                  pl.BlockSpec((PEER_HEADS, tt), lambda i, j: (0, i), pipeline_mode=once)],
        out_specs=pl.BlockSpec((D_MODEL, tt), lambda i, j: (0, i), pipeline_mode=once),
        out_shape=jax.ShapeDtypeStruct((D_MODEL, T_PAD), F32),
        scratch_shapes=[pltpu.VMEM((2, PEER_SUB, PEER_CHUNK), F32)] * 2,
        compiler_params=_params("parallel", "arbitrary"),
        name="peer_main",
    )(ht, u, vt, s1, e1, s2, e2, thr)


def kernel(x_prompt, x_sample, state_conv, state_ssm, w_in, conv_w, conv_b, dt_bias, a_log, d_skip,
           ssd_norm_w, sgu_ln_g, sgu_ln_b, sgu_w, sgu_b, w_branch_a, w_branch_b, w_out, ln1_g, ln1_b,
           peer_wq, peer_keys, peer_u, peer_v, ln2_g, ln2_b):
    row = lambda p: p[0].reshape(1, -1)
    x = jnp.concatenate([x_prompt.reshape(T_PROMPT, D_MODEL), x_sample.reshape(DEC_BATCH, D_MODEL)], axis=0)
    xb = x.astype(BF16)

    w = w_in[0]
    o1, o2, o3, o4 = D_INNER, D_INNER + CONV_DIM, D_INNER + CONV_DIM + HEADS, D_INNER + CONV_DIM + HEADS + 2 * D_GATE
    w_dt = jnp.pad(w[:, o2:o3], ((0, 0), (0, DT_PAD - HEADS)))
    mm = functools.partial(_matmul, tm=MM_TM)
    zs = mm(xb, w[:, :o1].astype(BF16), tn=MM_TN, act=_silu, name="proj_z")
    xbc = mm(xb, w[:, o1:o2].astype(BF16), tn=MM_TN, name="proj_xbc")
    dtr = mm(xb, w_dt.astype(BF16), tn=DT_PAD, name="proj_dt")
    uvg = mm(xb, w[:, o3:o4].astype(BF16), tn=MM_TN, act=_gelu, name="proj_uv")
    gates = mm(xb, w[:, o4:].astype(BF16), tn=MM_TN, act=jax.nn.sigmoid, name="proj_gates")

    pad_h = lambda p: jnp.pad(p[0], (0, DT_PAD - HEADS)).reshape(1, DT_PAD)
    dtb, alog = pad_h(dt_bias), pad_h(a_log)
    dexp = jnp.repeat(d_skip[0], HEADDIM).reshape(1, D_INNER)
    convw, convb, normw = conv_w[0], row(conv_b), row(ssd_norm_w)
    lng, lnb = row(sgu_ln_g), row(sgu_ln_b)

    sc_t = jnp.transpose(state_conv[0], (1, 0, 2))
    conv_s_t, xs_s, xdt_s, bm_s, cm_s, dec_s = _sample_pre(xbc, sc_t, dtr, convw, convb, dtb, alog)
    ssm_s, yt_s = _sample_state(dec_s, xdt_s.T.astype(BF16), bm_s, cm_s.T,
                                state_ssm[0].reshape(DEC_BATCH, D_INNER, STATE))

    ya, conv_p, ssm_p = _ssd(xbc, dtr, zs, convw, convb, dtb, alog, dexp, normw, yt_s.T, xs_s)
    bfull = jnp.repeat(sgu_b[0].T, CHUNK, axis=1)
    w0 = jnp.repeat(sgu_w[0][:, 0, 0], CHUNK).reshape(1, D_GATE)
    b0 = jnp.repeat(sgu_b[0][:, 0], CHUNK).reshape(1, D_GATE)
    yb, v_p, v_s = _sgu(uvg, lng, lnb, sgu_w[0], bfull, w0, b0)

    merged = _merge(ya, w_branch_a[0].astype(BF16), yb, w_branch_b[0].astype(BF16), gates)
    h, hb = _out_ln(merged, w_out[0].astype(BF16), x, row(ln1_g), row(ln1_b))

    ht = jnp.pad(hb.T, ((0, 0), (0, T_PAD - T_ALL)))
    qt = _matmul(peer_wq[0].T.astype(BF16), ht, tm=1024, tn=PEER_TT, out_dtype=BF16, name="peer_query")
    keys = peer_keys[0].reshape(2 * PEER_HEADS, PEER_KEYS, PEER_KEYS).astype(BF16)
    s1, e1, s2, e2, thr = _peer_front(qt, keys)
    p = _peer_main(ht, peer_u[0].astype(BF16), peer_v[0].T.astype(BF16), s1, e1, s2, e2, thr)
    y_prompt = _final_ln(h, p, row(ln2_g), row(ln2_b), tm=512, first_block=0, n_blocks=T_PROMPT // 512,
                         name="final_ln_prompt").reshape(BATCH, SEQ, D_MODEL)
    y_sample = _final_ln(h, p, row(ln2_g), row(ln2_b), tm=DEC_BATCH, first_block=SAMPLE_BLOCK, n_blocks=1,
                         name="final_ln_sample").reshape(DEC_BATCH, 1, D_MODEL)
    conv_sample = jnp.transpose(conv_s_t, (1, 0, 2))[None]
    return (y_prompt, y_sample, conv_p[None], ssm_p[None], v_p[None],
            conv_sample, ssm_s.reshape(1, DEC_BATCH, HEADS, HEADDIM, STATE), v_s[None, :, None, :])
```

```python
import functools
import math

import jax
import jax.numpy as jnp
from jax import lax
from jax.experimental import pallas as pl
from jax.experimental.pallas import tpu as pltpu

F32 = jnp.float32
BF16 = jnp.bfloat16

D_MODEL = 2048
SEQ = 2048
BATCH = 4
DEC_BATCH = 128
T_PROMPT = BATCH * SEQ
T_ALL = T_PROMPT + DEC_BATCH
CHUNK = 128
N_CHUNKS = SEQ // CHUNK
SAMPLE_BLOCK = T_PROMPT // CHUNK

D_INNER = 4096
HEADDIM = 64
HEADS = 64
GROUPS = 8
HEADS_PER_GROUP = HEADS // GROUPS
STATE = 128
CONV_K = 4
CONV_DIM = D_INNER + 2 * GROUPS * STATE
B_OFF = D_INNER
C_OFF = D_INNER + GROUPS * STATE
HEAD_PAIRS = HEADS // 2
DT_PAD = 128
D_GATE = 2048
MLP_GROUPS = 16
PEER_HEADS = 8
PEER_KEYS = 128
PEER_EXPERTS = PEER_KEYS * PEER_KEYS
PEER_TOPK = 16
DN_ALPHA = 2.0 ** 0.25
LN_EPS = 1e-5
RMS_EPS = 1e-5
NEG_INF = float("-inf")

OUT_LN_TM = 640
MM_TM = 2080
MERGE_TM = 1040
MM_TN = 1024
SAMPLES_PER_STEP = 4
PEER_EB = 2048
PEER_SUB = 512
PEER_CHUNK = 256
PEER_GATE_ROWS = 64
PEER_TT = 768
T_PAD = 8448
VMEM_LIMIT = 56 * 1024 * 1024


def _params(*sem):
    return pltpu.CompilerParams(dimension_semantics=sem, vmem_limit_bytes=VMEM_LIMIT)


def _gelu(x):
    return 0.5 * x * (1.0 + lax.erf(x * (1.0 / math.sqrt(2.0))))


def _silu(x):
    return x * jax.nn.sigmoid(x)


def _identity(x):
    return x


def _softplus(x):
    return jnp.maximum(x, 0.0) + jnp.log1p(jnp.exp(-jnp.abs(x)))


def _layer_norm(x, g, b):
    mu = jnp.mean(x, axis=-1, keepdims=True)
    xc = x - mu
    var = jnp.mean(xc * xc, axis=-1, keepdims=True)
    return xc * lax.rsqrt(var + LN_EPS) * g + b


def _dot(a, b):
    return jnp.dot(a, b, preferred_element_type=F32)


def _dot_f32(a, b):
    return jnp.dot(a, b, preferred_element_type=F32, precision=lax.Precision.HIGHEST)


def _mm_kernel(a_ref, b_ref, o_ref, *, act):
    o_ref[...] = act(_dot(a_ref[...], b_ref[...])).astype(o_ref.dtype)


def _matmul(a, b, *, tm, tn, name, act=_identity, out_dtype=F32):
    m, k = a.shape
    n = b.shape[1]
    return pl.pallas_call(
        functools.partial(_mm_kernel, act=act),
        grid=(m // tm, n // tn),
        in_specs=[pl.BlockSpec((tm, k), lambda i, j: (i, 0)),
                  pl.BlockSpec((k, tn), lambda i, j: (0, j))],
        out_specs=pl.BlockSpec((tm, tn), lambda i, j: (i, j)),
        out_shape=jax.ShapeDtypeStruct((m, n), out_dtype),
        compiler_params=_params("parallel", "parallel"),
        name=name,
    )(a, b)


def _ssd_gate_norm(y, xs, zs, dexp, normw):
    yg = (y + dexp * xs) * zs
    gw = D_INNER // GROUPS
    outs = []
    for g in range(GROUPS):
        blk = yg[:, g * gw:(g + 1) * gw]
        ms = jnp.mean(blk * blk, axis=-1, keepdims=True)
        outs.append(blk * lax.rsqrt(ms + RMS_EPS) * normw[:, g * gw:(g + 1) * gw])
    return outs


def _dot_exact01(x, r01):
    hi = x.astype(BF16)
    rest = x - hi.astype(F32)
    mid = rest.astype(BF16)
    lo = (rest - mid.astype(F32)).astype(BF16)
    return (_dot(hi, r01) + _dot(mid, r01)) + _dot(lo, r01)


def _ssd_kernel(xbc_ref, dtr_ref, zs_ref, convw_ref, convb_ref, dtb_ref, alog_ref,
                dexp_ref, normw_ref, ys_ref, xss_ref,
                y_ref, convst_ref, ssm_ref,
                xext, xc_s, rexp_s, rcol_s, colb_s, acumt_s, dtt_s, wt_s, stt_s, ysc_s):
    i = pl.program_id(0)
    gw = D_INNER // GROUPS

    @pl.when(i == 0)
    def _():
        hrow = lax.broadcasted_iota(jnp.int32, (DT_PAD, D_INNER), 0)
        col = lax.broadcasted_iota(jnp.int32, (DT_PAD, D_INNER), 1)
        rexp_s[...] = jnp.where(lax.shift_right_logical(col, 6) == hrow, 1.0, 0.0).astype(BF16)
        hrow = lax.broadcasted_iota(jnp.int32, (DT_PAD, HEADS * CHUNK), 0)
        col = lax.broadcasted_iota(jnp.int32, (DT_PAD, HEADS * CHUNK), 1)
        rcol_s[...] = jnp.where(lax.shift_right_logical(col, 7) == hrow, 1.0, 0.0).astype(BF16)

    @pl.when(i == BATCH * N_CHUNKS)
    def _():
        outs = _ssd_gate_norm(ys_ref[...], xss_ref[...], zs_ref[...], dexp_ref[...], normw_ref[...])
        for g in range(GROUPS):
            y_ref[:, g * gw:(g + 1) * gw] = outs[g].astype(y_ref.dtype)

    @pl.when(i < BATCH * N_CHUNKS)
    def _():
        _ssd_prompt_chunk(i % N_CHUNKS, xbc_ref, dtr_ref, zs_ref, convw_ref, convb_ref, dtb_ref, alog_ref,
                          dexp_ref, normw_ref, y_ref, convst_ref, ssm_ref,
                          xext, xc_s, rexp_s, rcol_s, colb_s, acumt_s, dtt_s, wt_s, stt_s, ysc_s)


def _ssd_prompt_chunk(c, xbc_ref, dtr_ref, zs_ref, convw_ref, convb_ref, dtb_ref, alog_ref,
                      dexp_ref, normw_ref, y_ref, convst_ref, ssm_ref,
                      xext, xc_s, rexp_s, rcol_s, colb_s, acumt_s, dtt_s, wt_s, stt_s, ysc_s):
    gw = D_INNER // GROUPS

    @pl.when(c == 0)
    def _():
        xext[0:8, :] = jnp.zeros((8, CONV_DIM), F32)
        stt_s[...] = jnp.zeros(stt_s.shape, F32)

    xext[8:8 + CHUNK, :] = xbc_ref[...]
    slab = 512
    for s0 in range(0, CONV_DIM, slab):
        sl = slice(s0, s0 + slab)
        ext = xext[:, sl]
        acc = convw_ref[0:1, sl] * ext
        acc = convw_ref[1:2, sl] * ext + pltpu.roll(acc, 1, axis=0)
        acc = convw_ref[2:3, sl] * ext + pltpu.roll(acc, 1, axis=0)
        acc = convw_ref[3:4, sl] * ext + pltpu.roll(acc, 1, axis=0)
        xc_s[:, sl] = _silu(convb_ref[:, sl] + acc[8:8 + CHUNK, :])
    convst_ref[0] = xbc_ref[CHUNK - 3:CHUNK, :]
    xext[0:8, :] = xbc_ref[CHUNK - 8:CHUNK, :]

    dtv = _softplus(dtr_ref[...] + dtb_ref[...])
    a = dtv * (-jnp.exp(alog_ref[...]))
    row = lax.broadcasted_iota(jnp.int32, (CHUNK, CHUNK), 0)
    colq = lax.broadcasted_iota(jnp.int32, (CHUNK, CHUNK), 1)
    causal = row >= colq
    acum = _dot_f32(jnp.where(causal, 1.0, 0.0).astype(F32), a)
    last = acum[CHUNK - 1:CHUNK, :]
    acumt_s[...] = acum.T
    dtt_s[...] = dtv.T
    wt_s[...] = (dtv * jnp.exp(last - acum)).T
    colb_s[...] = _dot_exact01(acum, rcol_s[...])
    chunk_decay = jnp.exp(_dot_exact01(jnp.broadcast_to(last, (8, DT_PAD)), rexp_s[...]))[0:1, :]

    lane = lax.broadcasted_iota(jnp.int32, (CHUNK, 128), 1)
    lo = lane < HEADDIM
    zero_b = jnp.zeros((CHUNK, 128), BF16)
    for g in range(GROUPS):
        bg = xc_s[:, B_OFF + g * STATE:B_OFF + (g + 1) * STATE]
        cg = xc_s[:, C_OFF + g * STATE:C_OFF + (g + 1) * STATE]
        cb = lax.dot_general(cg.astype(BF16), bg.astype(BF16), (((1,), (1,)), ((), ())),
                             preferred_element_type=F32)
        bgt = bg.T
        for pr in range(HEADS_PER_GROUP // 2):
            hp = g * (HEADS_PER_GROUP // 2) + pr
            psl = slice(hp * 128, (hp + 1) * 128)
            heads = (2 * hp, 2 * hp + 1)
            lhs = []
            for hh in heads:
                colb = colb_s[:, hh * CHUNK:(hh + 1) * CHUNK]
                seg = colb - acumt_s[hh:hh + 1, :]
                decay = jnp.exp(jnp.where(causal, seg, NEG_INF))
                lhs.append((cb * decay * dtt_s[hh:hh + 1, :]).astype(BF16))
            for hh in heads:
                colb = colb_s[:, hh * CHUNK:(hh + 1) * CHUNK]
                lhs.append((cg * jnp.exp(colb)).astype(BF16))
            xpair = xc_s[:, psl].astype(BF16)
            spair = stt_s[hp].astype(BF16)
            x_lo, x_hi = jnp.where(lo, xpair, zero_b), jnp.where(lo, zero_b, xpair)
            rhs = jnp.concatenate([x_lo, x_hi, jnp.where(lo, spair, zero_b), jnp.where(lo, zero_b, spair)],
                                  axis=0)
            ysc_s[:, psl] = _dot(jnp.concatenate(lhs, axis=1), rhs)
            bw = jnp.concatenate([(bgt * wt_s[hh:hh + 1, :]).astype(BF16) for hh in heads], axis=1)
            upd = _dot(bw, jnp.concatenate([x_lo, x_hi], axis=0))
            stt_s[hp] = stt_s[hp] * chunk_decay[:, psl] + upd

    outs = _ssd_gate_norm(ysc_s[...], xc_s[:, 0:D_INNER], zs_ref[...], dexp_ref[...], normw_ref[...])
    for g in range(GROUPS):
        y_ref[:, g * gw:(g + 1) * gw] = outs[g].astype(y_ref.dtype)

    @pl.when(c == N_CHUNKS - 1)
    def _():
        for hp in range(HEAD_PAIRS):
            st = stt_s[hp].T
            ssm_ref[0, 2 * hp] = st[0:HEADDIM, :]
            ssm_ref[0, 2 * hp + 1] = st[HEADDIM:2 * HEADDIM, :]


def _ssd(xbc, dtr, zs, convw, convb, dtb, alog, dexp, normw, y_sample, xs_sample):
    rowblk = lambda i: (i, 0)
    const = lambda i: (0, 0)
    seq = lambda i: jnp.minimum(i // N_CHUNKS, BATCH - 1)
    return pl.pallas_call(
        _ssd_kernel,
        grid=(BATCH * N_CHUNKS + 1,),
        in_specs=[pl.BlockSpec((CHUNK, CONV_DIM), rowblk),
                  pl.BlockSpec((CHUNK, DT_PAD), rowblk),
                  pl.BlockSpec((CHUNK, D_INNER), rowblk),
                  pl.BlockSpec((CONV_K, CONV_DIM), const),
                  pl.BlockSpec((1, CONV_DIM), const),
                  pl.BlockSpec((1, DT_PAD), const),
                  pl.BlockSpec((1, DT_PAD), const),
                  pl.BlockSpec((1, D_INNER), const),
                  pl.BlockSpec((1, D_INNER), const),
                  pl.BlockSpec((DEC_BATCH, D_INNER), const),
                  pl.BlockSpec((DEC_BATCH, D_INNER), const)],
        out_specs=[pl.BlockSpec((CHUNK, D_INNER), rowblk),
                   pl.BlockSpec((1, CONV_K - 1, CONV_DIM), lambda i: (seq(i), 0, 0)),
                   pl.BlockSpec((1, HEADS, HEADDIM, STATE), lambda i: (seq(i), 0, 0, 0))],
        out_shape=[jax.ShapeDtypeStruct((T_ALL, D_INNER), BF16),
                   jax.ShapeDtypeStruct((BATCH, CONV_K - 1, CONV_DIM), F32),
                   jax.ShapeDtypeStruct((BATCH, HEADS, HEADDIM, STATE), F32)],
        scratch_shapes=[pltpu.VMEM((8 + CHUNK, CONV_DIM), F32),
                        pltpu.VMEM((CHUNK, CONV_DIM), F32),
                        pltpu.VMEM((DT_PAD, D_INNER), BF16),
                        pltpu.VMEM((DT_PAD, HEADS * CHUNK), BF16),
                        pltpu.VMEM((CHUNK, HEADS * CHUNK), F32),
                        pltpu.VMEM((DT_PAD, CHUNK), F32),
                        pltpu.VMEM((DT_PAD, CHUNK), F32),
                        pltpu.VMEM((DT_PAD, CHUNK), F32),
                        pltpu.VMEM((HEAD_PAIRS, STATE, 128), F32),
                        pltpu.VMEM((CHUNK, D_INNER), F32)],
        compiler_params=_params("arbitrary"),
        name="ssd",
    )(xbc, dtr, zs, convw, convb, dtb, alog, dexp, normw, y_sample, xs_sample)


def _sgu_kernel(uvg_ref, lng_ref, lnb_ref, w_ref, bfull_ref, w0_ref, b0_ref, yb_ref, v_ref, vs_ref):
    i = pl.program_id(0)
    vn = _layer_norm(uvg_ref[:, D_GATE:2 * D_GATE], lng_ref[...], lnb_ref[...])

    @pl.when(i == BATCH * N_CHUNKS)
    def _():
        vs_ref[...] = vn
        s = w0_ref[...] * vn + b0_ref[...]
        yb_ref[...] = (uvg_ref[:, 0:D_GATE] * s).astype(yb_ref.dtype)

    @pl.when(i < BATCH * N_CHUNKS)
    def _():
        v_ref[0] = vn
        row = lax.broadcasted_iota(jnp.int32, (CHUNK, CHUNK), 0)
        col = lax.broadcasted_iota(jnp.int32, (CHUNK, CHUNK), 1)
        causal = row >= col
        for g in range(MLP_GROUPS):
            sl = slice(g * 128, (g + 1) * 128)
            wg = jnp.where(causal, w_ref[g], 0.0).astype(BF16)
            s = _dot(wg, vn[:, sl].astype(BF16)) + bfull_ref[:, sl]
            yb_ref[:, sl] = (uvg_ref[:, sl] * s).astype(yb_ref.dtype)


def _sgu(uvg, lng, lnb, w, bfull, w0, b0):
    const2 = lambda i: (0, 0)
    seq = lambda i: jnp.minimum(i // N_CHUNKS, BATCH - 1)
    return pl.pallas_call(
        _sgu_kernel,
        grid=(BATCH * N_CHUNKS + 1,),
        in_specs=[pl.BlockSpec((CHUNK, 2 * D_GATE), lambda i: (i, 0)),
                  pl.BlockSpec((1, D_GATE), const2),
                  pl.BlockSpec((1, D_GATE), const2),
                  pl.BlockSpec((MLP_GROUPS, CHUNK, CHUNK), lambda i: (0, 0, 0)),
                  pl.BlockSpec((CHUNK, D_GATE), const2),
                  pl.BlockSpec((1, D_GATE), const2),
                  pl.BlockSpec((1, D_GATE), const2)],
        out_specs=[pl.BlockSpec((CHUNK, D_GATE), lambda i: (i, 0)),
                   pl.BlockSpec((1, CHUNK, D_GATE), lambda i: (seq(i), 0, 0)),
                   pl.BlockSpec((DEC_BATCH, D_GATE), const2)],
        out_shape=[jax.ShapeDtypeStruct((T_ALL, D_GATE), BF16),
                   jax.ShapeDtypeStruct((BATCH, CHUNK, D_GATE), F32),
                   jax.ShapeDtypeStruct((DEC_BATCH, D_GATE), F32)],
        compiler_params=_params("arbitrary"),
        name="sgu",
    )(uvg, lng, lnb, w, bfull, w0, b0)


def _sample_pre_kernel(xbc_ref, sc_ref, dtr_ref, convw_ref, convb_ref, dtb_ref, alog_ref,
                       convnew_ref, xs_ref, xdt_ref, bm_ref, cm_ref, dec_ref):
    x = xbc_ref[...]
    acc = convb_ref[...] + convw_ref[3:4, :] * x
    acc = acc + convw_ref[2:3, :] * sc_ref[2]
    acc = acc + convw_ref[1:2, :] * sc_ref[1]
    acc = acc + convw_ref[0:1, :] * sc_ref[0]
    xc = _silu(acc)
    convnew_ref[0] = sc_ref[1]
    convnew_ref[1] = sc_ref[2]
    convnew_ref[2] = x
    dtv = _softplus(dtr_ref[...] + dtb_ref[...])
    dec_ref[...] = jnp.exp(dtv * (-jnp.exp(alog_ref[...])))
    hrow = lax.broadcasted_iota(jnp.int32, (DT_PAD, D_INNER), 0)
    col = lax.broadcasted_iota(jnp.int32, (DT_PAD, D_INNER), 1)
    rexp = jnp.where(lax.shift_right_logical(col, 6) == hrow, 1.0, 0.0).astype(F32)
    xs = xc[:, 0:D_INNER]
    xs_ref[...] = xs
    xdt_ref[...] = xs * _dot_f32(dtv, rexp)
    bm_ref[...] = xc[:, B_OFF:C_OFF]
    cm_ref[...] = xc[:, C_OFF:CONV_DIM]


def _sample_pre(xbc, sc_t, dtr, convw, convb, dtb, alog):
    blk = lambda i: (SAMPLE_BLOCK, 0)
    const = lambda i: (0, 0)
    return pl.pallas_call(
        _sample_pre_kernel,
        grid=(1,),
        in_specs=[pl.BlockSpec((DEC_BATCH, CONV_DIM), blk),
                  pl.BlockSpec((CONV_K - 1, DEC_BATCH, CONV_DIM), lambda i: (0, 0, 0)),
                  pl.BlockSpec((DEC_BATCH, DT_PAD), blk),
                  pl.BlockSpec((CONV_K, CONV_DIM), const),
                  pl.BlockSpec((1, CONV_DIM), const),
                  pl.BlockSpec((1, DT_PAD), const),
                  pl.BlockSpec((1, DT_PAD), const)],
        out_specs=[pl.BlockSpec((CONV_K - 1, DEC_BATCH, CONV_DIM), lambda i: (0, 0, 0)),
                   pl.BlockSpec((DEC_BATCH, D_INNER), const),
                   pl.BlockSpec((DEC_BATCH, D_INNER), const),
                   pl.BlockSpec((DEC_BATCH, GROUPS * STATE), const),
                   pl.BlockSpec((DEC_BATCH, GROUPS * STATE), const),
                   pl.BlockSpec((DEC_BATCH, DT_PAD), const)],
        out_shape=[jax.ShapeDtypeStruct((CONV_K - 1, DEC_BATCH, CONV_DIM), F32),
                   jax.ShapeDtypeStruct((DEC_BATCH, D_INNER), F32),
                   jax.ShapeDtypeStruct((DEC_BATCH, D_INNER), F32),
                   jax.ShapeDtypeStruct((DEC_BATCH, GROUPS * STATE), F32),
                   jax.ShapeDtypeStruct((DEC_BATCH, GROUPS * STATE), F32),
                   jax.ShapeDtypeStruct((DEC_BATCH, DT_PAD), F32)],
        compiler_params=_params("arbitrary"),
        name="sample_pre",
    )(xbc, sc_t, dtr, convw, convb, dtb, alog)


def _sample_state_kernel(dec_ref, xdtt_ref, bm_ref, cmt_ref, st_ref, new_ref, yt_ref):
    srow = lax.broadcasted_iota(jnp.int32, (DEC_BATCH, STATE), 0)
    slane = lax.broadcasted_iota(jnp.int32, (STATE, DEC_BATCH), 1)
    gw = HEADS_PER_GROUP * HEADDIM

    @pl.when(pl.program_id(0) == 0)
    def _():
        yt_ref[...] = jnp.zeros(yt_ref.shape, F32)

    for i in range(SAMPLES_PER_STEP):
        s = pl.program_id(0) * SAMPLES_PER_STEP + i
        brow = bm_ref[pl.ds(s, 1), :]
        for g in range(GROUPS):
            sel_b = jnp.where(srow == s, brow[:, g * STATE:(g + 1) * STATE], 0.0).astype(BF16)
            upd = _dot(xdtt_ref[g * gw:(g + 1) * gw, :], sel_b)
            news = []
            for r in range(HEADS_PER_GROUP):
                h = g * HEADS_PER_GROUP + r
                rows = slice(h * HEADDIM, (h + 1) * HEADDIM)
                new = st_ref[i, rows, :] * dec_ref[s, h] + upd[r * HEADDIM:(r + 1) * HEADDIM, :]
                new_ref[i, rows, :] = new
                news.append(new.astype(BF16))
            sel_c = jnp.where(slane == s, cmt_ref[g * STATE:(g + 1) * STATE, :], 0.0).astype(BF16)
            yt_ref[g * gw:(g + 1) * gw, :] += _dot(jnp.concatenate(news, axis=0), sel_c)


def _sample_state(dec, xdtt, bm, cmt, state):
    const = lambda s: (0, 0)
    blk = pl.BlockSpec((SAMPLES_PER_STEP, D_INNER, STATE), lambda s: (s, 0, 0))
    return pl.pallas_call(
        _sample_state_kernel,
        grid=(DEC_BATCH // SAMPLES_PER_STEP,),
        in_specs=[pl.BlockSpec(memory_space=pltpu.SMEM),
                  pl.BlockSpec((D_INNER, DEC_BATCH), const),
                  pl.BlockSpec((DEC_BATCH, GROUPS * STATE), const),
                  pl.BlockSpec((GROUPS * STATE, DEC_BATCH), const),
                  blk],
        out_specs=[blk, pl.BlockSpec((D_INNER, DEC_BATCH), const)],
        out_shape=[jax.ShapeDtypeStruct((DEC_BATCH, D_INNER, STATE), F32),
                   jax.ShapeDtypeStruct((D_INNER, DEC_BATCH), F32)],
        compiler_params=_params("arbitrary"),
        name="sample_state",
    )(dec, xdtt, bm, cmt, state)


def _merge_kernel(ya_ref, wa_ref, yb_ref, wb_ref, ga_ref, gb_ref, o_ref):
    o_ref[...] = (ga_ref[...] * _dot(ya_ref[...], wa_ref[...])
                  + gb_ref[...] * _dot(yb_ref[...], wb_ref[...])).astype(o_ref.dtype)


def _merge(ya, wa, yb, wb, gates):
    tm, tn = MERGE_TM, MM_TN // 2
    nb = D_MODEL // tn
    return pl.pallas_call(
        _merge_kernel,
        grid=(T_ALL // tm, nb),
        in_specs=[pl.BlockSpec((tm, D_INNER), lambda i, j: (i, 0)),
                  pl.BlockSpec((D_INNER, tn), lambda i, j: (0, j)),
                  pl.BlockSpec((tm, D_GATE), lambda i, j: (i, 0)),
                  pl.BlockSpec((D_GATE, tn), lambda i, j: (0, j)),
                  pl.BlockSpec((tm, tn), lambda i, j: (i, j)),
                  pl.BlockSpec((tm, tn), lambda i, j: (i, j + nb))],
        out_specs=pl.BlockSpec((tm, tn), lambda i, j: (i, j)),
        out_shape=jax.ShapeDtypeStruct((T_ALL, D_MODEL), BF16),
        compiler_params=_params("parallel", "parallel"),
        name="merge",
    )(ya, wa, yb, wb, gates, gates)


def _out_ln_kernel(m_ref, w_ref, x_ref, g_ref, b_ref, h_ref, hb_ref):
    h = _layer_norm(DN_ALPHA * x_ref[...] + _dot(m_ref[...], w_ref[...]), g_ref[...], b_ref[...])
    h_ref[...] = h
    hb_ref[...] = h.astype(BF16)


def _out_ln(merged, wout, x, g, b):
    tm = OUT_LN_TM
    const = lambda i: (0, 0)
    rows = lambda i: (i, 0)
    return pl.pallas_call(
        _out_ln_kernel,
        grid=(T_ALL // tm,),
        in_specs=[pl.BlockSpec((tm, D_MODEL), rows),
                  pl.BlockSpec((D_MODEL, D_MODEL), const),
                  pl.BlockSpec((tm, D_MODEL), rows),
                  pl.BlockSpec((1, D_MODEL), const),
                  pl.BlockSpec((1, D_MODEL), const)],
        out_specs=[pl.BlockSpec((tm, D_MODEL), rows), pl.BlockSpec((tm, D_MODEL), rows)],
        out_shape=[jax.ShapeDtypeStruct((T_ALL, D_MODEL), F32),
                   jax.ShapeDtypeStruct((T_ALL, D_MODEL), BF16)],
        compiler_params=_params("parallel"),
        name="out_ln",
    )(merged, wout, x, g, b)


def _final_ln_kernel(h_ref, pt_ref, g_ref, b_ref, o_ref):
    o_ref[...] = _layer_norm(DN_ALPHA * h_ref[...] + pt_ref[...].T, g_ref[...], b_ref[...])


def _final_ln(h, pt, g, b, *, tm, first_block, n_blocks, name):
    const = lambda i: (0, 0)
    rows = lambda i: (first_block + i, 0)
    return pl.pallas_call(
        _final_ln_kernel,
        grid=(n_blocks,),
        in_specs=[pl.BlockSpec((tm, D_MODEL), rows),
                  pl.BlockSpec((D_MODEL, tm), lambda i: (0, first_block + i)),
                  pl.BlockSpec((1, D_MODEL), const), pl.BlockSpec((1, D_MODEL), const)],
        out_specs=pl.BlockSpec((tm, D_MODEL), lambda i: (i, 0)),
        out_shape=jax.ShapeDtypeStruct((n_blocks * tm, D_MODEL), F32),
        compiler_params=_params("parallel"),
        name=name,
    )(h, pt, g, b)


def _compare_exchange(v, i, l, descending):
    hi, lo = jnp.maximum(v[i], v[l]), jnp.minimum(v[i], v[l])
    v[i], v[l] = (hi, lo) if descending else (lo, hi)


def _bitonic_merge_desc(v):
    v = list(v)
    j = len(v) // 2
    while j >= 1:
        for i in range(len(v)):
            if i ^ j > i:
                _compare_exchange(v, i, i ^ j, True)
        j //= 2
    return v


def _bitonic_sort_desc(v):
    v = list(v)
    k = 2
    while k <= len(v):
        j = k // 2
        while j >= 1:
            for i in range(len(v)):
                if i ^ j > i:
                    _compare_exchange(v, i, i ^ j, (i & k) == 0)
            j //= 2
        k *= 2
    return v


def _merge_sublanes_top16(v):
    for shift in (4, 6, 7):
        rolled = [pltpu.roll(a, shift, axis=0) for a in v]
        v = _bitonic_merge_desc([jnp.maximum(v[i], rolled[15 - i]) for i in range(16)])
    return [a[0:1, :] for a in v]


def _top16_of_rows(x):
    return _merge_sublanes_top16(_bitonic_sort_desc([x[8 * i:8 * i + 8, :] for i in range(16)]))


def _top16_of_list(rows):
    lanes = rows[0].shape[1]
    rows = rows + [jnp.full((1, lanes), NEG_INF, F32)] * (64 - len(rows))
    v = _bitonic_sort_desc([jnp.concatenate(rows[8 * i:8 * i + 8], axis=0) for i in range(8)])
    rolled = [pltpu.roll(a, 4, axis=0) for a in v]
    v = _bitonic_merge_desc(v + rolled[::-1])
    for shift in (6, 7):
        rolled = [pltpu.roll(a, shift, axis=0) for a in v]
        v = _bitonic_merge_desc([jnp.maximum(v[i], rolled[15 - i]) for i in range(16)])
    return [a[0:1, :] for a in v]


def _peer_front_kernel(qt_ref, keys_ref, tau_ref, e1_ref, s2_ref, e2_ref):
    k = PEER_TOPK
    for h in range(PEER_HEADS):
        sc = []
        top = []
        for side in range(2):
            hs = 2 * h + side
            x = _dot(keys_ref[hs], qt_ref[hs * PEER_KEYS:(hs + 1) * PEER_KEYS, :])
            sc.append(x)
            top.append(_top16_of_rows(x))
        best = _top16_of_list([top[0][i] + top[1][j] for i in range(k) for j in range(k)
                               if (i + 1) * (j + 1) <= k])
        z = jnp.zeros_like(best[0])
        for bk in best:
            z = z + jnp.exp(bk - best[0])
        thr = best[k - 1]
        b_all = jnp.concatenate(top[1], axis=0)
        tau = jnp.full(sc[0].shape, jnp.inf, F32)
        for a in top[0]:
            passing = jnp.where(a + b_all >= thr, b_all, jnp.inf)
            tau = jnp.where(sc[0] == a, jnp.min(passing, axis=0, keepdims=True), tau)
        tau_ref[h] = tau
        s2_ref[h] = sc[1]
        e1_ref[h] = jnp.exp(sc[0] - top[0][0]) * (1.0 / z)
        e2_ref[h] = jnp.exp(sc[1] - top[1][0])


def _peer_front(qt, keys):
    tt = 128
    blk3 = pl.BlockSpec((PEER_HEADS, PEER_KEYS, tt), lambda i: (0, 0, i))
    shp3 = jax.ShapeDtypeStruct((PEER_HEADS, PEER_KEYS, T_PAD), F32)
    return pl.pallas_call(
        _peer_front_kernel,
        grid=(T_PAD // tt,),
        in_specs=[pl.BlockSpec((2 * PEER_HEADS * PEER_KEYS, tt), lambda i: (0, i)),
                  pl.BlockSpec((2 * PEER_HEADS, PEER_KEYS, PEER_KEYS), lambda i: (0, 0, 0))],
        out_specs=[blk3, blk3, blk3, blk3],
        out_shape=[shp3, shp3, shp3, shp3],
        compiler_params=_params("parallel"),
        name="peer_front",
    )(qt, keys)


def _peer_main_kernel(ht_ref, u_ref, vt_ref, tau_ref, e1_ref, s2_ref, e2_ref, o_ref, *act_s):
    j = pl.program_id(1)
    tt = ht_ref.shape[1]
    n_i1 = PEER_SUB // PEER_KEYS
    hk = PEER_GATE_ROWS
    n_chunks = tt // PEER_CHUNK
    chains = [(sub, c) for sub in range(PEER_EB // PEER_SUB) for c in range(n_chunks)]

    @pl.when(j == 0)
    def _():
        o_ref[...] = jnp.zeros(o_ref.shape, F32)

    slot = lax.rem(j, 2)

    def activations(k):
        sub, c = chains[k]
        act_s[k % 2][slot] = _dot(u_ref[sub * PEER_SUB:(sub + 1) * PEER_SUB, :],
                                  ht_ref[:, c * PEER_CHUNK:(c + 1) * PEER_CHUNK])
        return act_s[k % 2].at[slot]

    def gating(k, act, i1_list):
        sub, c = chains[k]
        tok_tiles = []
        for tc in range(PEER_CHUNK // 128):
            cols = slice(c * PEER_CHUNK + tc * 128, c * PEER_CHUNK + (tc + 1) * 128)
            lcols = slice(tc * 128, (tc + 1) * 128)
            parts = []
            for part in range(PEER_KEYS // hk):
                krows = slice(part * hk, (part + 1) * hk)
                gates = {r: jnp.zeros((hk, 128), F32) for r in i1_list}
                for h in range(PEER_HEADS):
                    s2t = s2_ref[h, krows, cols]
                    e2t = e2_ref[h, krows, cols]
                    taut = tau_ref[h, :, cols]
                    e1t = e1_ref[h, :, cols]
                    for r in i1_list:
                        q = sub * n_i1 + r
                        w = e1t[q:q + 1, :] * e2t
                        gates[r] = gates[r] + jnp.where(s2t >= taut[q:q + 1, :], w, 0.0)
                for r in i1_list:
                    arows = slice(r * PEER_KEYS + part * hk, r * PEER_KEYS + (part + 1) * hk)
                    gates[r] = gates[r] * _gelu(act[arows, lcols])
                parts.append(gates)
            tiles = [jnp.concatenate([p[r] for p in parts], axis=0).astype(BF16) for r in i1_list]
            tok_tiles.append(jnp.concatenate(tiles, axis=0))
        return jnp.concatenate(tok_tiles, axis=1)

    def accumulate(k, gt):
        sub, c = chains[k]
        o_ref[:, c * PEER_CHUNK:(c + 1) * PEER_CHUNK] += _dot(
            vt_ref[:, sub * PEER_SUB:(sub + 1) * PEER_SUB], gt)

    act = activations(0)
    for k in range(len(chains)):
        act_next = activations(k + 1) if k + 1 < len(chains) else None
        accumulate(k, gating(k, act, tuple(range(n_i1))))
        act = act_next


def _peer_main(ht, u, vt, tau, e1, s2, e2):
    tt, eb = PEER_TT, PEER_EB
    once = pl.Buffered(1)
    blk3 = pl.BlockSpec((PEER_HEADS, PEER_KEYS, tt), lambda i, j: (0, 0, i), pipeline_mode=once)
    blk1 = pl.BlockSpec((PEER_HEADS, eb // PEER_KEYS, tt), lambda i, j: (0, j, i))
    return pl.pallas_call(
        _peer_main_kernel,
        grid=(T_PAD // tt, PEER_EXPERTS // eb),
        in_specs=[pl.BlockSpec((D_MODEL, tt), lambda i, j: (0, i), pipeline_mode=once),
                  pl.BlockSpec((eb, D_MODEL), lambda i, j: (j, 0)),
                  pl.BlockSpec((D_MODEL, eb), lambda i, j: (0, j)),
                  blk1, blk1, blk3, blk3],
        out_specs=pl.BlockSpec((D_MODEL, tt), lambda i, j: (0, i), pipeline_mode=once),
        out_shape=jax.ShapeDtypeStruct((D_MODEL, T_PAD), F32),
        scratch_shapes=[pltpu.VMEM((2, PEER_SUB, PEER_CHUNK), F32)] * 2,
        compiler_params=_params("parallel", "arbitrary"),
        name="peer_main",
    )(ht, u, vt, tau, e1, s2, e2)


def kernel(x_prompt, x_sample, state_conv, state_ssm, w_in, conv_w, conv_b, dt_bias, a_log, d_skip,
           ssd_norm_w, sgu_ln_g, sgu_ln_b, sgu_w, sgu_b, w_branch_a, w_branch_b, w_out, ln1_g, ln1_b,
           peer_wq, peer_keys, peer_u, peer_v, ln2_g, ln2_b):
    row = lambda p: p[0].reshape(1, -1)
    x = jnp.concatenate([x_prompt.reshape(T_PROMPT, D_MODEL), x_sample.reshape(DEC_BATCH, D_MODEL)], axis=0)
    xb = x.astype(BF16)

    w = w_in[0]
    o1, o2, o3, o4 = D_INNER, D_INNER + CONV_DIM, D_INNER + CONV_DIM + HEADS, D_INNER + CONV_DIM + HEADS + 2 * D_GATE
    w_dt = jnp.pad(w[:, o2:o3], ((0, 0), (0, DT_PAD - HEADS)))
    mm = functools.partial(_matmul, tm=MM_TM)
    zs = mm(xb, w[:, :o1].astype(BF16), tn=MM_TN, act=_silu, name="proj_z")
    xbc = mm(xb, w[:, o1:o2].astype(BF16), tn=MM_TN, name="proj_xbc")
    dtr = mm(xb, w_dt.astype(BF16), tn=DT_PAD, name="proj_dt")
    uvg = mm(xb, w[:, o3:o4].astype(BF16), tn=MM_TN, act=_gelu, name="proj_uv")
    gates = mm(xb, w[:, o4:].astype(BF16), tn=MM_TN, act=jax.nn.sigmoid, name="proj_gates")

    pad_h = lambda p: jnp.pad(p[0], (0, DT_PAD - HEADS)).reshape(1, DT_PAD)
    dtb, alog = pad_h(dt_bias), pad_h(a_log)
    dexp = jnp.repeat(d_skip[0], HEADDIM).reshape(1, D_INNER)
    convw, convb, normw = conv_w[0], row(conv_b), row(ssd_norm_w)
    lng, lnb = row(sgu_ln_g), row(sgu_ln_b)

    sc_t = jnp.transpose(state_conv[0], (1, 0, 2))
    conv_s_t, xs_s, xdt_s, bm_s, cm_s, dec_s = _sample_pre(xbc, sc_t, dtr, convw, convb, dtb, alog)
    ssm_s, yt_s = _sample_state(dec_s, xdt_s.T.astype(BF16), bm_s, cm_s.T,
                                state_ssm[0].reshape(DEC_BATCH, D_INNER, STATE))

    ya, conv_p, ssm_p = _ssd(xbc, dtr, zs, convw, convb, dtb, alog, dexp, normw, yt_s.T, xs_s)
    bfull = jnp.repeat(sgu_b[0].T, CHUNK, axis=1)
    w0 = jnp.repeat(sgu_w[0][:, 0, 0], CHUNK).reshape(1, D_GATE)
    b0 = jnp.repeat(sgu_b[0][:, 0], CHUNK).reshape(1, D_GATE)
    yb, v_p, v_s = _sgu(uvg, lng, lnb, sgu_w[0], bfull, w0, b0)

    merged = _merge(ya, w_branch_a[0].astype(BF16), yb, w_branch_b[0].astype(BF16), gates)
    h, hb = _out_ln(merged, w_out[0].astype(BF16), x, row(ln1_g), row(ln1_b))

    ht = jnp.pad(hb.T, ((0, 0), (0, T_PAD - T_ALL)))
    qt = _matmul(peer_wq[0].T.astype(BF16), ht, tm=1024, tn=PEER_TT, out_dtype=BF16, name="peer_query")
    keys = peer_keys[0].reshape(2 * PEER_HEADS, PEER_KEYS, PEER_KEYS).astype(BF16)
    tau, e1, s2, e2 = _peer_front(qt, keys)
    p = _peer_main(ht, peer_u[0].astype(BF16), peer_v[0].T.astype(BF16), tau, e1, s2, e2)
    y_prompt = _final_ln(h, p, row(ln2_g), row(ln2_b), tm=512, first_block=0, n_blocks=T_PROMPT // 512,
                         name="final_ln_prompt").reshape(BATCH, SEQ, D_MODEL)
    y_sample = _final_ln(h, p, row(ln2_g), row(ln2_b), tm=DEC_BATCH, first_block=SAMPLE_BLOCK, n_blocks=1,
                         name="final_ln_sample").reshape(DEC_BATCH, 1, D_MODEL)
    conv_sample = jnp.transpose(conv_s_t, (1, 0, 2))[None]
    return (y_prompt, y_sample, conv_p[None], ssm_p[None], v_p[None],
            conv_sample, ssm_s.reshape(1, DEC_BATCH, HEADS, HEADDIM, STATE), v_s[None, :, None, :])
```
